```python
import jax, jax.numpy as jnp
from jax import lax
import numpy as np

D_MODEL = 1024
BATCH = 16
SEQ = 2048
DEPTH = 4
DEC_BATCH = 128
DEC_SEQ = 4
PAST_LEN = 8192
PAGE_SIZE = 128

N_HEADS = 8
QK_NOPE_DIM = 64
QK_ROPE_DIM = 32
V_HEAD_DIM = 64
Q_LORA_RANK = 384
KV_LORA_RANK = 256
ROPE_THETA = 10000.0
ATTN_SCALE = (QK_NOPE_DIM + QK_ROPE_DIM) ** -0.5
ATTN_BLOCK = 128
BRANCH_W = N_HEADS * V_HEAD_DIM
CONV_W = BRANCH_W
CONV_K = 3
SGU_W = BRANCH_W
SGU_GROUPS = 8
SGU_GROUP_DIM = SGU_W // SGU_GROUPS
CHUNK = 128
N_BRANCH = 3
FFN_DIM = 2816
EPS = 1e-6
IN_W = Q_LORA_RANK + KV_LORA_RANK + QK_ROPE_DIM + 3 * CONV_W + 2 * SGU_W + N_BRANCH * D_MODEL

kernel_name = 'hybrid_mla_shortconv_sgu_decoder_step'


def rms_norm(x, g):
    x32 = x.astype(jnp.float32)
    y = x32 * lax.rsqrt(jnp.mean(x32 * x32, axis=-1, keepdims=True) + EPS)
    return (y * g.astype(jnp.float32)).astype(x.dtype)


def layer_norm(x, g, b):
    x32 = x.astype(jnp.float32)
    mu = jnp.mean(x32, axis=-1, keepdims=True)
    xc = x32 - mu
    y = xc * lax.rsqrt(jnp.mean(xc * xc, axis=-1, keepdims=True) + EPS)
    return (y * g.astype(jnp.float32) + b.astype(jnp.float32)).astype(x.dtype)


def rope(x, pos):
    half = QK_ROPE_DIM // 2
    inv = ROPE_THETA ** (-jnp.arange(half, dtype=jnp.float32) / half)
    ang = pos.astype(jnp.float32)[:, None] * inv[None, :]
    ang = ang.reshape((ang.shape[0],) + (1,) * (x.ndim - 3) + (half,))
    cos = jnp.cos(ang).astype(x.dtype)
    sin = jnp.sin(ang).astype(x.dtype)
    x1, x2 = x[..., :half], x[..., half:]
    return jnp.concatenate([x1 * cos - x2 * sin, x1 * sin + x2 * cos], axis=-1)


def swiglu_ffn(x, g, w_gate_up, w_down):
    h = rms_norm(x, g) @ w_gate_up
    a, b = h[..., :FFN_DIM], h[..., FFN_DIM:]
    return (jax.nn.silu(a) * b) @ w_down


def split_in(z):
    sizes = (Q_LORA_RANK, KV_LORA_RANK, QK_ROPE_DIM, CONV_W, CONV_W, CONV_W, SGU_W, SGU_W)
    idx, acc = [], 0
    for s in sizes:
        acc += s
        idx.append(acc)
    return jnp.split(z, idx, axis=-1)


def attn_scores(q_lat, q_rope, c, kr):
    s = jnp.einsum('bqhc,bkc->bhqk', q_lat, c) + jnp.einsum('bqhr,bkr->bhqk', q_rope, kr)
    return s.astype(jnp.float32) * ATTN_SCALE


def mla_prompt(q_lat, q_rope, c_kv, k_rope):
    B, S = q_lat.shape[:2]
    nb = S // ATTN_BLOCK
    ql = q_lat.reshape(B, nb, ATTN_BLOCK, N_HEADS, KV_LORA_RANK).transpose(1, 0, 2, 3, 4)
    qr = q_rope.reshape(B, nb, ATTN_BLOCK, N_HEADS, QK_ROPE_DIM).transpose(1, 0, 2, 3, 4)
    k_pos = jnp.arange(S)

    def block(args):
        i, qlb, qrb = args
        s = attn_scores(qlb, qrb, c_kv, k_rope)
        q_pos = i * ATTN_BLOCK + jnp.arange(ATTN_BLOCK)
        s = jnp.where(k_pos[None, :] <= q_pos[:, None], s, -jnp.inf)
        p = jax.nn.softmax(s, axis=-1).astype(c_kv.dtype)
        return jnp.einsum('bhqk,bkc->bqhc', p, c_kv)

    o = lax.map(block, (jnp.arange(nb), ql, qr))
    return o.transpose(1, 0, 2, 3, 4).reshape(B, S, N_HEADS, KV_LORA_RANK)


def mla_sample(q_lat, q_rope, c_new, kr_new, c_past, kr_past):
    T = q_lat.shape[1]
    P = c_past.shape[1]
    s_past = attn_scores(q_lat, q_rope, c_past, kr_past)
    s_new = attn_scores(q_lat, q_rope, c_new, kr_new)
    s_new = jnp.where(jnp.tril(jnp.ones((T, T), dtype=bool)), s_new, -jnp.inf)
    p = jax.nn.softmax(jnp.concatenate([s_past, s_new], axis=-1), axis=-1).astype(c_new.dtype)
    return (jnp.einsum('bhqk,bkc->bqhc', p[..., :P], c_past)
            + jnp.einsum('bhqk,bkc->bqhc', p[..., P:], c_new))


def short_conv(b_gate, c_gate, x_c, conv_state, conv_w):
    u = c_gate * x_c
    padded = jnp.concatenate([conv_state, u], axis=1)
    S = u.shape[1]
    y = conv_w[0] * padded[:, 0:S]
    for k in range(1, CONV_K):
        y = y + conv_w[k] * padded[:, k:k + S]
    return b_gate * y, padded[:, -(CONV_K - 1):]


def chunk_sgu(u, v, ln_g, ln_b, w_spatial, b_spatial):
    u = jax.nn.gelu(u)
    v = layer_norm(jax.nn.gelu(v), ln_g, ln_b)
    B, S = v.shape[:2]
    L = min(S, CHUNK)
    nc = S // L
    vc = v.reshape(B, nc, L, SGU_GROUPS, SGU_GROUP_DIM)
    w = jnp.tril(w_spatial[:, :L, :L])
    mix = jnp.einsum('gts,bnsgc->bntgc', w, vc) + b_spatial[:, :L].T[:, :, None]
    return u * mix.reshape(B, S, SGU_W), v


def token_mixer(x, pos, conv_state, attend, mix_norm, w_in, gate_bias, q_norm, w_uq, kv_norm,
                w_uk, w_uv, conv_w, sgu_ln_g, sgu_ln_b, sgu_w_spatial, sgu_b_spatial, w_branch, w_out):
    h = rms_norm(x, mix_norm)
    B, S = h.shape[:2]
    c_q, c_kv, k_r, cb, cc, cx, su, sv, gate_logits = split_in(h @ w_in)
    q = jnp.einsum('bsr,rhd->bshd', rms_norm(c_q, q_norm), w_uq)
    q_rope = rope(q[..., QK_NOPE_DIM:], pos)
    q_lat = jnp.einsum('bshn,chn->bshc', q[..., :QK_NOPE_DIM], w_uk)
    c_kv = rms_norm(c_kv, kv_norm)
    k_rope = rope(k_r, pos)
    o_lat = attend(q_lat, q_rope, c_kv, k_rope)
    out_a = jnp.einsum('bshc,chv->bshv', o_lat, w_uv).reshape(B, S, BRANCH_W)
    out_b, new_conv = short_conv(cb, cc, cx, conv_state, conv_w)
    out_c, v_rows = chunk_sgu(su, sv, sgu_ln_g, sgu_ln_b, sgu_w_spatial, sgu_b_spatial)
    branches = jnp.stack([out_a, out_b, out_c], axis=2)
    proj = jnp.einsum('bsic,icd->bsid', branches, w_branch)
    gates = jax.nn.sigmoid(gate_logits.reshape(B, S, N_BRANCH, D_MODEL) + gate_bias)
    y = jnp.sum(gates * proj, axis=2) @ w_out
    return y, c_kv, k_rope, new_conv, v_rows


def setup_inputs(seed: int = 0) -> dict:
    key = jax.random.key(seed)
    ks = jax.random.split(key, 32)
    f32 = jnp.float32

    def nrm(k, shape, scale):
        return jax.random.normal(k, shape, f32) * scale

    n_pages = PAST_LEN // PAGE_SIZE
    n_used = DEC_BATCH * n_pages
    n_pool = n_used + max(1, n_used // 4)
    page_table = jax.random.permutation(ks[5], n_pool)[:n_used].reshape(DEC_BATCH, n_pages).astype(jnp.int32)
    return {
        'x_prompt': nrm(ks[0], (BATCH, SEQ, D_MODEL), 1.0),
        'x_sample': nrm(ks[1], (DEC_BATCH, DEC_SEQ, D_MODEL), 1.0),
        'cache_kv_latent': nrm(ks[2], (DEPTH, n_pool, PAGE_SIZE, KV_LORA_RANK), 1.0),
        'cache_k_rope': nrm(ks[3], (DEPTH, n_pool, PAGE_SIZE, QK_ROPE_DIM), 1.0),
        'state_conv': nrm(ks[4], (DEPTH, DEC_BATCH, CONV_K - 1, CONV_W), 1.0),
        'page_table': page_table,
        'ffn1_norm': 1.0 + nrm(ks[6], (DEPTH, D_MODEL), 0.02),
        'ffn1_w_gate_up': nrm(ks[7], (DEPTH, D_MODEL, 2 * FFN_DIM), D_MODEL ** -0.5),
        'ffn1_w_down': nrm(ks[8], (DEPTH, FFN_DIM, D_MODEL), FFN_DIM ** -0.5),
        'mix_norm': 1.0 + nrm(ks[9], (DEPTH, D_MODEL), 0.02),
        'w_in': nrm(ks[10], (DEPTH, D_MODEL, IN_W), D_MODEL ** -0.5),
        'gate_bias': nrm(ks[11], (DEPTH, N_BRANCH, D_MODEL), 0.02),
        'q_norm': 1.0 + nrm(ks[12], (DEPTH, Q_LORA_RANK), 0.02),
        'w_uq': nrm(ks[13], (DEPTH, Q_LORA_RANK, N_HEADS, QK_NOPE_DIM + QK_ROPE_DIM), Q_LORA_RANK ** -0.5),
        'kv_norm': 1.0 + nrm(ks[14], (DEPTH, KV_LORA_RANK), 0.02),
        'w_uk': nrm(ks[15], (DEPTH, KV_LORA_RANK, N_HEADS, QK_NOPE_DIM), KV_LORA_RANK ** -0.5),
        'w_uv': nrm(ks[16], (DEPTH, KV_LORA_RANK, N_HEADS, V_HEAD_DIM), KV_LORA_RANK ** -0.5),
        'conv_w': nrm(ks[17], (DEPTH, CONV_K, CONV_W), CONV_K ** -0.5),
        'sgu_ln_g': 1.0 + nrm(ks[18], (DEPTH, SGU_W), 0.02),
        'sgu_ln_b': nrm(ks[19], (DEPTH, SGU_W), 0.02),
        'sgu_w_spatial': nrm(ks[20], (DEPTH, SGU_GROUPS, CHUNK, CHUNK), CHUNK ** -0.5),
        'sgu_b_spatial': 1.0 + nrm(ks[21], (DEPTH, SGU_GROUPS, CHUNK), 0.02),
        'w_branch': nrm(ks[22], (DEPTH, N_BRANCH, BRANCH_W, D_MODEL), BRANCH_W ** -0.5),
        'w_out': nrm(ks[23], (DEPTH, D_MODEL, D_MODEL), D_MODEL ** -0.5),
        'ffn2_norm': 1.0 + nrm(ks[24], (DEPTH, D_MODEL), 0.02),
        'ffn2_w_gate_up': nrm(ks[25], (DEPTH, D_MODEL, 2 * FFN_DIM), D_MODEL ** -0.5),
        'ffn2_w_down': nrm(ks[26], (DEPTH, FFN_DIM, D_MODEL), FFN_DIM ** -0.5),
        'final_norm': 1.0 + nrm(ks[27], (D_MODEL,), 0.02),
    }


def reference(x_prompt, x_sample, cache_kv_latent, cache_k_rope, state_conv, page_table,
              ffn1_norm, ffn1_w_gate_up, ffn1_w_down, mix_norm, w_in, gate_bias, q_norm, w_uq,
              kv_norm, w_uk, w_uv, conv_w, sgu_ln_g, sgu_ln_b, sgu_w_spatial, sgu_b_spatial,
              w_branch, w_out, ffn2_norm, ffn2_w_gate_up, ffn2_w_down, final_norm):
    Bp, S = x_prompt.shape[:2]
    Bs, T = x_sample.shape[:2]
    past_len = page_table.shape[1] * cache_kv_latent.shape[2]
    pos_p = jnp.arange(S)
    pos_s = past_len + jnp.arange(T)
    zero_conv = jnp.zeros((Bp, CONV_K - 1, CONV_W), x_prompt.dtype)

    xp, xs = x_prompt, x_sample
    lat_p, kr_p, conv_p = [], [], []
    lat_s, kr_s, conv_s, v_s = [], [], [], []
    for l in range(DEPTH):
        lw = (mix_norm[l], w_in[l], gate_bias[l], q_norm[l], w_uq[l], kv_norm[l], w_uk[l], w_uv[l],
              conv_w[l], sgu_ln_g[l], sgu_ln_b[l], sgu_w_spatial[l], sgu_b_spatial[l], w_branch[l], w_out[l])
        xp = xp + 0.5 * swiglu_ffn(xp, ffn1_norm[l], ffn1_w_gate_up[l], ffn1_w_down[l])
        mix, c_kv, k_rope, new_conv, _ = token_mixer(xp, pos_p, zero_conv, mla_prompt, *lw)
        xp = xp + mix
        xp = xp + 0.5 * swiglu_ffn(xp, ffn2_norm[l], ffn2_w_gate_up[l], ffn2_w_down[l])
        lat_p.append(c_kv)
        kr_p.append(k_rope)
        conv_p.append(new_conv)
        c_past = cache_kv_latent[l][page_table].reshape(Bs, past_len, KV_LORA_RANK)
        kr_past = cache_k_rope[l][page_table].reshape(Bs, past_len, QK_ROPE_DIM)

        def attend_s(ql, qr, c, kr, c_past=c_past, kr_past=kr_past):
            return mla_sample(ql, qr, c, kr, c_past, kr_past)

        xs = xs + 0.5 * swiglu_ffn(xs, ffn1_norm[l], ffn1_w_gate_up[l], ffn1_w_down[l])
        mix, c_kv, k_rope, new_conv, v_rows = token_mixer(xs, pos_s, state_conv[l], attend_s, *lw)
        xs = xs + mix
        xs = xs + 0.5 * swiglu_ffn(xs, ffn2_norm[l], ffn2_w_gate_up[l], ffn2_w_down[l])
        lat_s.append(c_kv)
        kr_s.append(k_rope)
        conv_s.append(new_conv)
        v_s.append(v_rows)

    y_prompt = rms_norm(xp, final_norm)
    y_sample = rms_norm(xs, final_norm)
    return (y_prompt, y_sample, jnp.stack(lat_p), jnp.stack(kr_p), jnp.stack(conv_p),
            jnp.stack(lat_s), jnp.stack(kr_s), jnp.stack(conv_s), jnp.stack(v_s))
```

```python
import functools
import math

import jax
import jax.numpy as jnp
from jax import lax
from jax.experimental import pallas as pl
from jax.experimental.pallas import tpu as pltpu

F32 = jnp.float32
MXU_DTYPE = jnp.bfloat16

EPS = 1e-6
ROPE_THETA = 10000.0
SGU_CHUNK = 128
LANES = 128
SUBLANES = 8
VMEM_LIMIT_BYTES = 56 * 1024 * 1024

FFN_ROW_TILE = 512
FFN_COL_CHUNK = 256
MIX_ROW_TILE = 256
OUT_ROW_TILE = 512
ATTN_Q_TILE = 256


def _dot(a, b):
    return jnp.dot(a, b, preferred_element_type=F32)


def _dot_nt(a, b):
    return lax.dot_general(a, b, (((1,), (1,)), ((), ())), preferred_element_type=F32)


def _rms(x):
    return x * lax.rsqrt(jnp.mean(x * x, axis=-1, keepdims=True) + EPS)


def _const_spec(shape):
    zeros = (0,) * len(shape)
    return pl.BlockSpec(shape, lambda *_: zeros, pipeline_mode=pl.Buffered(1))


def _params(n_grid_axes):
    return pltpu.CompilerParams(
        dimension_semantics=("arbitrary",) * n_grid_axes,
        vmem_limit_bytes=VMEM_LIMIT_BYTES)


def _ffn_body(*refs, ffn_dim, chunk, final):
    if final:
        x_ref, g_ref, wgu_ref, wd_ref, fg_ref, o_ref, act_ref = refs
    else:
        x_ref, g_ref, wgu_ref, wd_ref, o_ref, act_ref = refs
    x = x_ref[...]
    xn = (_rms(x) * g_ref[...]).astype(MXU_DTYPE)
    for c in range(ffn_dim // chunk):
        a = _dot(xn, wgu_ref[:, c * chunk:(c + 1) * chunk])
        b = _dot(xn, wgu_ref[:, ffn_dim + c * chunk:ffn_dim + (c + 1) * chunk])
        act_ref[:, c * chunk:(c + 1) * chunk] = (a * jax.nn.sigmoid(a) * b).astype(MXU_DTYPE)
    out = x + 0.5 * _dot(act_ref[...], wd_ref[...])
    if final:
        out = _rms(out) * fg_ref[...]
    o_ref[...] = out


def _ffn(x, g, wgu, wd, final_g=None):
    n, d = x.shape
    ffn_dim = wd.shape[0]
    tm = min(FFN_ROW_TILE, n)
    assert n % tm == 0 and ffn_dim % FFN_COL_CHUNK == 0
    final = final_g is not None
    in_specs = [pl.BlockSpec((tm, d), lambda i: (i, 0)),
                _const_spec((1, d)), _const_spec(wgu.shape), _const_spec(wd.shape)]
    args = [x, g.reshape(1, d), wgu, wd]
    if final:
        in_specs.append(_const_spec((1, d)))
        args.append(final_g.reshape(1, d))
    return pl.pallas_call(
        functools.partial(_ffn_body, ffn_dim=ffn_dim, chunk=FFN_COL_CHUNK, final=final),
        grid=(n // tm,),
        in_specs=in_specs,
        out_specs=pl.BlockSpec((tm, d), lambda i: (i, 0)),
        out_shape=jax.ShapeDtypeStruct((n, d), F32),
        scratch_shapes=[pltpu.VMEM((tm, ffn_dim), MXU_DTYPE)],
        compiler_params=_params(1),
        name="ffn_final" if final else "ffn",
    )(*args)


def _rope_rotate(x, cos, sin_signed):
    half = 16
    lane = lax.broadcasted_iota(jnp.int32, x.shape, 1)
    up = pltpu.roll(x, LANES - half, 1)
    down = pltpu.roll(x, half, 1)
    partner = jnp.where((lane % (2 * half)) < half, up, down)
    return x * cos + partner * sin_signed


def _mix_in_body(*refs, dims, sample, tiles_per_seq, q_scale):
    (d_model, q_rank, kv_rank, rope_dim, conv_w, sgu_w, n_heads, n_groups) = dims
    it = iter(refs)
    x_ref, cos_ref, sin_ref = next(it), next(it), next(it)
    if sample:
        s0_ref, s1_ref, tab_ref, btab_ref = next(it), next(it), next(it), next(it)
    else:
        wsp_ref, bsp_ref = next(it), next(it)
    (mixn_ref, win_ref, qn_ref, wqn_ref, wqr_ref, wuk_ref, kvn_ref, cw_ref, lng_ref, lnb_ref,
     wbb_ref, wbc_ref, gb_ref) = [next(it) for _ in range(13)]
    q_out, k_out, ckv_out, kr_out, pbc_out, ga_out = [next(it) for _ in range(6)]
    if sample:
        u_out, v_out = next(it), next(it)
        ubuf, vbuf = next(it), next(it)
    else:
        nc_out = next(it)
        ubuf = next(it)

    tm = x_ref.shape[0]
    halo = SUBLANES
    j = pl.program_id(0) % tiles_per_seq

    o_q = 0
    o_kv = o_q + q_rank
    o_cb = o_kv + kv_rank
    o_cc = o_cb + conv_w
    o_cx = o_cc + conv_w
    o_su = o_cx + conv_w
    o_sv = o_su + sgu_w
    o_g = o_sv + sgu_w
    o_kr = o_g + 3 * d_model

    def proj(lo, width):
        return _dot(xn, win_ref[:, lo:lo + width])

    xn = (_rms(x_ref[...]) * mixn_ref[...]).astype(MXU_DTYPE)
    cos = cos_ref[...]
    sin = sin_ref[...]

    cq = (_rms(proj(o_q, q_rank)) * qn_ref[...]).astype(MXU_DTYPE)
    q_nope = _dot(cq, wqn_ref[...]).astype(MXU_DTYPE)
    q_rope = _dot(cq, wqr_ref[...])
    lat_w = 2 * kv_rank
    qw = kv_rank + LANES
    heads_per_block = LANES // rope_dim
    lane = lax.broadcasted_iota(jnp.int32, (tm, LANES), 1)
    rot = [_rope_rotate(q_rope[:, b * LANES:(b + 1) * LANES], cos, sin) * q_scale
           for b in range(n_heads // heads_per_block)]
    for p in range(n_heads // 2):
        ql = _dot(q_nope[:, p * LANES:(p + 1) * LANES], wuk_ref[p]) * q_scale
        for e in range(2):
            h = 2 * p + e
            q_out[:, h * qw:h * qw + kv_rank] = ql[:, e * kv_rank:(e + 1) * kv_rank].astype(q_out.dtype)
            blk, pos = divmod(h, heads_per_block)
            r = rot[blk]
            if pos:
                r = pltpu.roll(r, LANES - pos * rope_dim, 1)
            r = jnp.where(lane < rope_dim, r, 0.0)
            q_out[:, h * qw + kv_rank:(h + 1) * qw] = r.astype(q_out.dtype)

    ckv = _rms(proj(o_kv, kv_rank)) * kvn_ref[...]
    ckv_out[...] = ckv
    kr = _rope_rotate(proj(o_kr, LANES), cos, sin)
    kr_out[...] = kr[:, :rope_dim]
    k_out[:, :kv_rank] = ckv.astype(k_out.dtype)
    k_out[:, kv_rank:] = kr.astype(k_out.dtype)

    u = proj(o_cc, conv_w) * proj(o_cx, conv_w)
    if sample:
        ubuf[0:halo, :] = jnp.zeros((halo, conv_w), F32)
    else:
        @pl.when(j == 0)
        def _():
            ubuf[0:halo, :] = jnp.zeros((halo, conv_w), F32)
    ubuf[halo:halo + tm, :] = u
    u1 = ubuf[halo - 1:halo - 1 + tm, :]
    u2 = ubuf[halo - 2:halo - 2 + tm, :]
    if sample:
        t_len = tab_ref.shape[0]
        t = lax.broadcasted_iota(jnp.int32, (tm, conv_w), 0) % t_len
        u1 = jnp.where(t == 0, s1_ref[...], u1)
        u2 = jnp.where(t == 0, s0_ref[...], jnp.where(t == 1, s1_ref[...], u2))
        u_out[...] = u
    else:
        ubuf[0:halo, :] = u[tm - halo:, :]

        @pl.when(j == tiles_per_seq - 1)
        def _():
            nc_out[0] = u[tm - 2:, :]
    cw = cw_ref[...]
    out_b = proj(o_cb, conv_w) * (cw[0:1] * u2 + cw[1:2] * u1 + cw[2:3] * u)

    gu = jax.nn.gelu(proj(o_su, sgu_w))
    gv = jax.nn.gelu(proj(o_sv, sgu_w))
    mu = jnp.mean(gv, axis=-1, keepdims=True)
    gc = gv - mu
    v = gc * lax.rsqrt(jnp.mean(gc * gc, axis=-1, keepdims=True) + EPS) * lng_ref[...] + lnb_ref[...]
    if sample:
        v_out[...] = v
        t_len = tab_ref.shape[0]
        vbuf[0:halo, :] = jnp.zeros((halo, sgu_w), F32)
        vbuf[halo:halo + tm, :] = v
        reps = tm // SUBLANES
        mix = btab_ref[...][None] + tab_ref[0][None] * v.reshape(reps, SUBLANES, sgu_w)
        for dlt in range(1, t_len):
            vs = vbuf[halo - dlt:halo - dlt + tm, :].reshape(reps, SUBLANES, sgu_w)
            mix = mix + tab_ref[dlt][None] * vs
        mix = mix.reshape(tm, sgu_w)
    else:
        gd = sgu_w // n_groups
        gpb = LANES // gd
        row = lax.broadcasted_iota(jnp.int32, (SGU_CHUNK, SGU_CHUNK), 0)
        col = lax.broadcasted_iota(jnp.int32, (SGU_CHUNK, SGU_CHUNK), 1)
        wt = [jnp.where(row >= col, wsp_ref[g], 0.0).astype(MXU_DTYPE) for g in range(n_groups)]
        clane = lax.broadcasted_iota(jnp.int32, (SGU_CHUNK, LANES), 1)
        vb16 = v.astype(MXU_DTYPE)
        chunks = []
        for ch in range(tm // SGU_CHUNK):
            blocks = []
            for b in range(sgu_w // LANES):
                vb = vb16[ch * SGU_CHUNK:(ch + 1) * SGU_CHUNK, b * LANES:(b + 1) * LANES]
                res = _dot(wt[b * gpb], vb)
                for e in range(1, gpb):
                    res = jnp.where(clane >= e * gd, _dot(wt[b * gpb + e], vb), res)
                blocks.append(res)
            chunks.append(jnp.concatenate(blocks, axis=-1) + bsp_ref[...])
        mix = jnp.concatenate(chunks, axis=0)
    out_c = gu * mix

    gb = gb_ref[...]
    g_a = jax.nn.sigmoid(proj(o_g, d_model) + gb[0:1])
    g_b = jax.nn.sigmoid(proj(o_g + d_model, d_model) + gb[1:2])
    g_c = jax.nn.sigmoid(proj(o_g + 2 * d_model, d_model) + gb[2:3])
    ga_out[...] = g_a
    pbc_out[...] = (g_b * _dot(out_b.astype(MXU_DTYPE), wbb_ref[...])
                    + g_c * _dot(out_c.astype(MXU_DTYPE), wbc_ref[...]))


def _mix_in(x, cos, sin, w, dims, seq_len, q_scale, sample_extra=None):
    (d_model, q_rank, kv_rank, rope_dim, conv_w, sgu_w, n_heads, n_groups) = dims
    n = x.shape[0]
    sample = sample_extra is not None
    if sample:
        tm = min(MIX_ROW_TILE, n)
        assert tm % seq_len == 0 and SUBLANES % seq_len == 0 and seq_len >= 2
        tiles_per_seq = 1
        tab_index = lambda i: (i, 0)
    else:
        tm = min(MIX_ROW_TILE, seq_len)
        assert seq_len % tm == 0 and tm % SGU_CHUNK == 0
        tiles_per_seq = seq_len // tm
        tab_index = lambda i: (i % tiles_per_seq, 0)
    assert n % tm == 0 and LANES % rope_dim == 0 and n_heads % (LANES // rope_dim) == 0
    assert n_heads % 2 == 0 and rope_dim == 32 and LANES % (sgu_w // n_groups) == 0
    qw = kv_rank + LANES
    row = lambda width: pl.BlockSpec((tm, width), lambda i: (i, 0))

    in_specs = [row(d_model), pl.BlockSpec((tm, LANES), tab_index), pl.BlockSpec((tm, LANES), tab_index)]
    args = [x, cos, sin]
    if sample:
        s0, s1, tab, btab = sample_extra
        in_specs += [row(conv_w), row(conv_w), _const_spec(tab.shape), _const_spec(btab.shape)]
        args += [s0, s1, tab, btab]
    else:
        in_specs += [_const_spec(w["wsp"].shape), _const_spec(w["bsp"].shape)]
        args += [w["wsp"], w["bsp"]]
    names = ["mix_norm", "w_in", "q_norm", "wq_nope", "wq_rope", "wuk_pair", "kv_norm", "conv_w",
             "ln_g", "ln_b", "wb_b", "wb_c", "gate_bias"]
    in_specs += [_const_spec(w[k].shape) for k in names]
    args += [w[k] for k in names]

    out_shape = [jax.ShapeDtypeStruct((n, n_heads * qw), MXU_DTYPE),
                 jax.ShapeDtypeStruct((n, qw), MXU_DTYPE),
                 jax.ShapeDtypeStruct((n, kv_rank), F32),
                 jax.ShapeDtypeStruct((n, rope_dim), F32),
                 jax.ShapeDtypeStruct((n, d_model), F32),
                 jax.ShapeDtypeStruct((n, d_model), F32)]
    out_specs = [row(n_heads * qw), row(qw), row(kv_rank), row(rope_dim), row(d_model), row(d_model)]
    scratch = [pltpu.VMEM((tm + SUBLANES, conv_w), F32)]
    if sample:
        out_shape += [jax.ShapeDtypeStruct((n, conv_w), F32), jax.ShapeDtypeStruct((n, sgu_w), F32)]
        out_specs += [row(conv_w), row(sgu_w)]
        scratch.append(pltpu.VMEM((tm + SUBLANES, sgu_w), F32))
    else:
        n_seq = n // seq_len
        out_shape.append(jax.ShapeDtypeStruct((n_seq, 2, conv_w), F32))
        out_specs.append(pl.BlockSpec((1, 2, conv_w), lambda i: (i // tiles_per_seq, 0, 0)))
    return pl.pallas_call(
        functools.partial(_mix_in_body, dims=dims, sample=sample, tiles_per_seq=tiles_per_seq,
                          q_scale=q_scale),
        grid=(n // tm,),
        in_specs=in_specs,
        out_specs=out_specs,
        out_shape=out_shape,
        scratch_shapes=scratch,
        compiler_params=_params(1),
        name="mix_in_sample" if sample else "mix_in_prompt",
    )(*args)


def _attn_prompt_body(q_ref, k_ref, o_ref, m_ref, l_ref, acc_ref, *, heads, kv_rank, kb):
    i = pl.program_id(1)
    q = q_ref[0]
    rows = q.shape[0]
    m_ref[...] = jnp.full(m_ref.shape, -jnp.inf, F32)
    l_ref[...] = jnp.zeros(l_ref.shape, F32)
    acc_ref[...] = jnp.zeros(acc_ref.shape, F32)

    def step(k_start, diagonal):
        k = k_ref[0, pl.ds(k_start, kb), :]
        s = _dot_nt(q, k)
        if diagonal:
            r = lax.broadcasted_iota(jnp.int32, (rows, kb), 0)
            c = lax.broadcasted_iota(jnp.int32, (rows, kb), 1)
            s = jnp.where(c * heads <= r, s, -jnp.inf)
        m_prev = m_ref[...]
        m_new = jnp.maximum(m_prev, jnp.max(s, axis=-1, keepdims=True))
        alpha = jnp.exp2(m_prev - m_new)
        p = jnp.exp2(s - m_new)
        l_ref[...] = alpha * l_ref[...] + jnp.sum(p, axis=-1, keepdims=True)
        acc_ref[...] = alpha * acc_ref[...] + _dot(p.astype(k.dtype), k[:, :kv_rank])
        m_ref[...] = m_new

    def full_block(jb, carry):
        step(pl.multiple_of(jb * kb, kb), False)
        return carry

    lax.fori_loop(0, i, full_block, 0)
    step(pl.multiple_of(i * kb, kb), True)
    o_ref[0] = (acc_ref[...] / l_ref[...]).astype(o_ref.dtype)


def _attn_prompt(q, k, heads, kv_rank):
    b, s, qw = k.shape
    kb = min(ATTN_Q_TILE, s)
    assert s % kb == 0
    rows = kb * heads
    return pl.pallas_call(
        functools.partial(_attn_prompt_body, heads=heads, kv_rank=kv_rank, kb=kb),
        grid=(b, s // kb),
        in_specs=[pl.BlockSpec((1, rows, qw), lambda bi, i: (bi, i, 0)),
                  pl.BlockSpec((1, s, qw), lambda bi, i: (bi, 0, 0))],
        out_specs=pl.BlockSpec((1, rows, kv_rank), lambda bi, i: (bi, i, 0)),
        out_shape=jax.ShapeDtypeStruct((b, s * heads, kv_rank), MXU_DTYPE),
        scratch_shapes=[pltpu.VMEM((rows, 1), F32), pltpu.VMEM((rows, 1), F32),
                        pltpu.VMEM((rows, kv_rank), F32)],
        compiler_params=_params(2),
        name="attn_prompt",
    )(q, k)


def _attn_sample_body(pt_ref, q_ref, kn_ref, lat_hbm, rope_hbm, o_ref, cbuf, rbuf, sems,
                      *, layer, heads, kv_rank, rope_dim):
    b = pl.program_id(0)
    nb = pl.num_programs(0)
    n_pages = cbuf.shape[1]
    page = cbuf.shape[2]

    def copies(bi, slot, pg):
        src = pt_ref[bi, pg]
        return (pltpu.make_async_copy(lat_hbm.at[layer, src], cbuf.at[slot, pg], sems.at[0, slot]),
                pltpu.make_async_copy(rope_hbm.at[layer, src], rbuf.at[slot, pg], sems.at[1, slot]))

    def start_all(bi, slot):
        for pg in range(n_pages):
            for cp in copies(bi, slot, pg):
                cp.start()

    def wait_all(bi, slot):
        for pg in range(n_pages):
            for cp in copies(bi, slot, pg):
                cp.wait()

    slot = b % 2

    @pl.when(b == 0)
    def _():
        start_all(0, 0)

    @pl.when(b + 1 < nb)
    def _():
        start_all(b + 1, 1 - slot)

    wait_all(b, slot)

    q = q_ref[0]
    rows = q.shape[0]
    c = cbuf[slot].reshape(n_pages * page, kv_rank).astype(q.dtype)
    kr = rbuf[slot].reshape(n_pages * page, rope_dim).astype(q.dtype)
    s = _dot_nt(q[:, :kv_rank], c) + _dot_nt(q[:, kv_rank:kv_rank + rope_dim], kr)
    kn = kn_ref[0]
    t_new = kn.shape[0]
    s_new = _dot_nt(q, kn)
    r = lax.broadcasted_iota(jnp.int32, (rows, t_new), 0)
    cc = lax.broadcasted_iota(jnp.int32, (rows, t_new), 1)
    s_new = jnp.where(cc * heads <= r, s_new, -jnp.inf)
    m = jnp.maximum(jnp.max(s, axis=-1, keepdims=True), jnp.max(s_new, axis=-1, keepdims=True))
    p = jnp.exp2(s - m)
    p_new = jnp.exp2(s_new - m)
    l = jnp.sum(p, axis=-1, keepdims=True) + jnp.sum(p_new, axis=-1, keepdims=True)
    o = _dot(p.astype(q.dtype), c) + _dot(p_new.astype(q.dtype), kn[:, :kv_rank])
    o_ref[0] = (o / l).astype(o_ref.dtype)


def _attn_sample(page_table, q, k_new, cache_lat, cache_rope, layer, heads, kv_rank):
    bs, rows, qw = q.shape
    t_new = k_new.shape[1]
    n_pages = page_table.shape[1]
    page = cache_lat.shape[2]
    rope_dim = cache_rope.shape[3]
    grid_spec = pltpu.PrefetchScalarGridSpec(
        num_scalar_prefetch=1,
        grid=(bs,),
        in_specs=[pl.BlockSpec((1, rows, qw), lambda b, pt: (b, 0, 0)),
                  pl.BlockSpec((1, t_new, qw), lambda b, pt: (b, 0, 0)),
                  pl.BlockSpec(memory_space=pl.ANY),
                  pl.BlockSpec(memory_space=pl.ANY)],
        out_specs=pl.BlockSpec((1, rows, kv_rank), lambda b, pt: (b, 0, 0)),
        scratch_shapes=[pltpu.VMEM((2, n_pages, page, kv_rank), F32),
                        pltpu.VMEM((2, n_pages, page, rope_dim), F32),
                        pltpu.SemaphoreType.DMA((2, 2))])
    return pl.pallas_call(
        functools.partial(_attn_sample_body, layer=layer, heads=heads, kv_rank=kv_rank,
                          rope_dim=rope_dim),
        grid_spec=grid_spec,
        out_shape=jax.ShapeDtypeStruct((bs, rows, kv_rank), MXU_DTYPE),
        compiler_params=_params(1),
        name="attn_sample",
    )(page_table, q, k_new, cache_lat, cache_rope)


def _mix_out_body(o_ref, x_ref, pbc_ref, ga_ref, wuv_ref, wba_ref, wout_ref, out_ref, *, pair_w):
    n_pairs = wuv_ref.shape[0]
    parts = [_dot(o_ref[:, p * pair_w:(p + 1) * pair_w], wuv_ref[p]) for p in range(n_pairs)]
    out_a = jnp.concatenate(parts, axis=-1).astype(MXU_DTYPE)
    merged = pbc_ref[...] + ga_ref[...] * _dot(out_a, wba_ref[...])
    out_ref[...] = x_ref[...] + _dot(merged.astype(MXU_DTYPE), wout_ref[...])


def _mix_out(o_lat, x, pbc, ga, wuv_pair, wb_a, w_out):
    n, d = x.shape
    tm = min(OUT_ROW_TILE, n)
    assert n % tm == 0
    pair_w = wuv_pair.shape[1]
    row = lambda width: pl.BlockSpec((tm, width), lambda i: (i, 0))
    return pl.pallas_call(
        functools.partial(_mix_out_body, pair_w=pair_w),
        grid=(n // tm,),
        in_specs=[row(o_lat.shape[1]), row(d), row(d), row(d),
                  _const_spec(wuv_pair.shape), _const_spec(wb_a.shape), _const_spec(w_out.shape)],
        out_specs=row(d),
        out_shape=jax.ShapeDtypeStruct((n, d), F32),
        compiler_params=_params(1),
        name="mix_out",
    )(o_lat, x, pbc, ga, wuv_pair, wb_a, w_out)


def _pair_block_diag(t):
    l, h, r, c = t.shape
    t = t.reshape(l, h // 2, 2, r, c)
    z = jnp.zeros((l, h // 2, r, c), t.dtype)
    top = jnp.concatenate([t[:, :, 0], z], axis=-1)
    bot = jnp.concatenate([z, t[:, :, 1]], axis=-1)
    return jnp.concatenate([top, bot], axis=-2)


def _rope_tables(pos, rope_dim):
    half = rope_dim // 2
    inv = ROPE_THETA ** (-jnp.arange(half, dtype=F32) / half)
    ang = pos.astype(F32)[:, None] * inv[None, :]
    cos, sin = jnp.cos(ang), jnp.sin(ang)
    reps = LANES // rope_dim
    return (jnp.tile(jnp.concatenate([cos, cos], axis=-1), (1, reps)),
            jnp.tile(jnp.concatenate([-sin, sin], axis=-1), (1, reps)))


def kernel(x_prompt, x_sample, cache_kv_latent, cache_k_rope, state_conv, page_table, ffn1_norm, ffn1_w_gate_up, ffn1_w_down, mix_norm, w_in, gate_bias, q_norm, w_uq, kv_norm, w_uk, w_uv, conv_w, sgu_ln_g, sgu_ln_b, sgu_w_spatial, sgu_b_spatial, w_branch, w_out, ffn2_norm, ffn2_w_gate_up, ffn2_w_down, final_norm):
    bp, s, d_model = x_prompt.shape
    bs, t_new, _ = x_sample.shape
    depth = w_in.shape[0]
    q_rank = q_norm.shape[-1]
    kv_rank = kv_norm.shape[-1]
    rope_dim = cache_k_rope.shape[-1]
    n_heads, nope_dim = w_uk.shape[2], w_uk.shape[3]
    v_dim = w_uv.shape[3]
    cw_dim = conv_w.shape[-1]
    conv_k = conv_w.shape[1]
    sgu_w = sgu_ln_g.shape[-1]
    n_groups = sgu_w_spatial.shape[1]
    group_dim = sgu_w // n_groups
    n_branch = w_branch.shape[1]
    past_len = page_table.shape[1] * cache_kv_latent.shape[2]
    assert conv_k == 3 and n_branch == 3 and s >= SGU_CHUNK and t_new < SGU_CHUNK
    assert w_uq.shape[3] == nope_dim + rope_dim and 2 * nope_dim == LANES and 2 * v_dim == LANES
    dims = (d_model, q_rank, kv_rank, rope_dim, cw_dim, sgu_w, n_heads, n_groups)
    cdt = MXU_DTYPE

    o_kr = q_rank + kv_rank
    w_in_p = jnp.concatenate(
        [w_in[..., :o_kr], w_in[..., o_kr + rope_dim:], w_in[..., o_kr:o_kr + rope_dim],
         jnp.zeros((depth, d_model, LANES - rope_dim), w_in.dtype)], axis=-1).astype(cdt)
    q_scale = (nope_dim + rope_dim) ** -0.5 * math.log2(math.e)
    wq_nope = w_uq[..., :nope_dim].reshape(depth, q_rank, n_heads * nope_dim).astype(cdt)
    wq_rope = w_uq[..., nope_dim:].reshape(depth, q_rank, n_heads * rope_dim).astype(cdt)
    wuk_pair = _pair_block_diag(w_uk.transpose(0, 2, 3, 1)).astype(cdt)
    wuv_pair = _pair_block_diag(w_uv.transpose(0, 2, 1, 3)).astype(cdt)
    wb = w_branch.astype(cdt)
    w_out_c = w_out.astype(cdt)
    f1gu, f1d = ffn1_w_gate_up.astype(cdt), ffn1_w_down.astype(cdt)
    f2gu, f2d = ffn2_w_gate_up.astype(cdt), ffn2_w_down.astype(cdt)

    bsp = jnp.repeat(sgu_b_spatial[:, :, :SGU_CHUNK].transpose(0, 2, 1), group_dim, axis=-1)
    w4 = jnp.tril(sgu_w_spatial[:, :, :t_new, :t_new])
    tabs = []
    for dlt in range(t_new):
        diag = jnp.diagonal(w4, offset=-dlt, axis1=2, axis2=3)
        diag = jnp.pad(diag, ((0, 0), (0, 0), (dlt, 0)))
        tabs.append(jnp.repeat(diag.transpose(0, 2, 1), group_dim, axis=-1))
    reps8 = SUBLANES // t_new
    tab_s = jnp.tile(jnp.stack(tabs, axis=1), (1, 1, reps8, 1))
    btab_s = jnp.tile(jnp.repeat(sgu_b_spatial[:, :, :t_new].transpose(0, 2, 1), group_dim, axis=-1),
                      (1, reps8, 1))

    cos_p, sin_p = _rope_tables(jnp.arange(s), rope_dim)
    cos_s, sin_s = _rope_tables(past_len + jnp.arange(t_new), rope_dim)
    cos_s, sin_s = jnp.tile(cos_s, (bs, 1)), jnp.tile(sin_s, (bs, 1))

    xp = x_prompt.reshape(bp * s, d_model)
    xs = x_sample.reshape(bs * t_new, d_model)
    qw = kv_rank + LANES
    lat_p, kr_p, conv_p, lat_s, kr_s, conv_s, v_s = [], [], [], [], [], [], []
    for l in range(depth):
        w = {"mix_norm": mix_norm[l].reshape(1, -1), "w_in": w_in_p[l], "q_norm": q_norm[l].reshape(1, -1),
             "wq_nope": wq_nope[l], "wq_rope": wq_rope[l], "wuk_pair": wuk_pair[l],
             "kv_norm": kv_norm[l].reshape(1, -1), "conv_w": conv_w[l],
             "ln_g": sgu_ln_g[l].reshape(1, -1), "ln_b": sgu_ln_b[l].reshape(1, -1),
             "wb_b": wb[l, 1], "wb_c": wb[l, 2], "gate_bias": gate_bias[l],
             "wsp": sgu_w_spatial[l][:, :SGU_CHUNK, :SGU_CHUNK], "bsp": bsp[l]}
        last = l == depth - 1

        xp = _ffn(xp, ffn1_norm[l], f1gu[l], f1d[l])
        q, kk, ckv, kr, pbc, ga, nconv = _mix_in(xp, cos_p, sin_p, w, dims, s, q_scale)
        o = _attn_prompt(q.reshape(bp, s * n_heads, qw), kk.reshape(bp, s, qw), n_heads, kv_rank)
        xp = _mix_out(o.reshape(bp * s, n_heads * kv_rank), xp, pbc, ga, wuv_pair[l], wb[l, 0], w_out_c[l])
        xp = _ffn(xp, ffn2_norm[l], f2gu[l], f2d[l], final_norm if last else None)
        lat_p.append(ckv.reshape(bp, s, kv_rank))
        kr_p.append(kr.reshape(bp, s, rope_dim))
        conv_p.append(nconv)

        st = state_conv[l]
        s0 = jnp.repeat(st[:, 0], t_new, axis=0)
        s1 = jnp.repeat(st[:, 1], t_new, axis=0)
        xs = _ffn(xs, ffn1_norm[l], f1gu[l], f1d[l])
        q, kk, ckv, kr, pbc, ga, u, v = _mix_in(xs, cos_s, sin_s, w, dims, t_new, q_scale,
                                                sample_extra=(s0, s1, tab_s[l], btab_s[l]))
        o = _attn_sample(page_table, q.reshape(bs, t_new * n_heads, qw), kk.reshape(bs, t_new, qw),
                         cache_kv_latent, cache_k_rope, l, n_heads, kv_rank)
        xs = _mix_out(o.reshape(bs * t_new, n_heads * kv_rank), xs, pbc, ga, wuv_pair[l], wb[l, 0], w_out_c[l])
        xs = _ffn(xs, ffn2_norm[l], f2gu[l], f2d[l], final_norm if last else None)
        lat_s.append(ckv.reshape(bs, t_new, kv_rank))
        kr_s.append(kr.reshape(bs, t_new, rope_dim))
        conv_s.append(u.reshape(bs, t_new, cw_dim)[:, t_new - (conv_k - 1):])
        v_s.append(v.reshape(bs, t_new, sgu_w))

    return (xp.reshape(bp, s, d_model), xs.reshape(bs, t_new, d_model),
            jnp.stack(lat_p), jnp.stack(kr_p), jnp.stack(conv_p),
            jnp.stack(lat_s), jnp.stack(kr_s), jnp.stack(conv_s), jnp.stack(v_s))
```

```python
import functools
import math

import jax
import jax.numpy as jnp
from jax import lax
from jax.experimental import pallas as pl
from jax.experimental.pallas import tpu as pltpu

F32 = jnp.float32
MXU_DTYPE = jnp.bfloat16

EPS = 1e-6
ROPE_THETA = 10000.0
SGU_CHUNK = 128
LANES = 128
SUBLANES = 8
VMEM_LIMIT_BYTES = 56 * 1024 * 1024

FFN_ROW_TILE = 512
FFN_COL_CHUNK = 256
MIX_ROW_TILE = 256
OUT_ROW_TILE = 512
ATTN_Q_TILE = 256


def _dot(a, b):
    return jnp.dot(a, b, preferred_element_type=F32)


def _dot_nt(a, b):
    return lax.dot_general(a, b, (((1,), (1,)), ((), ())), preferred_element_type=F32)


def _rms(x):
    return x * lax.rsqrt(jnp.mean(x * x, axis=-1, keepdims=True) + EPS)


def _const_spec(shape):
    zeros = (0,) * len(shape)
    return pl.BlockSpec(shape, lambda *_: zeros, pipeline_mode=pl.Buffered(1))


def _params(n_grid_axes):
    return pltpu.CompilerParams(
        dimension_semantics=("arbitrary",) * n_grid_axes,
        vmem_limit_bytes=VMEM_LIMIT_BYTES)


def _ffn_body(*refs, ffn_dim, chunk, final):
    if final:
        x_ref, g_ref, wgu_ref, wd_ref, fg_ref, o_ref, act_ref = refs
    else:
        x_ref, g_ref, wgu_ref, wd_ref, o_ref, act_ref = refs
    x = x_ref[...]
    xn = (_rms(x) * g_ref[...]).astype(MXU_DTYPE)
    for c in range(ffn_dim // chunk):
        a = _dot(xn, wgu_ref[:, c * chunk:(c + 1) * chunk])
        b = _dot(xn, wgu_ref[:, ffn_dim + c * chunk:ffn_dim + (c + 1) * chunk])
        act_ref[:, c * chunk:(c + 1) * chunk] = (a * jax.nn.sigmoid(a) * b).astype(MXU_DTYPE)
    out = x + 0.5 * _dot(act_ref[...], wd_ref[...])
    if final:
        out = _rms(out) * fg_ref[...]
    o_ref[...] = out


def _ffn(x, g, wgu, wd, final_g=None):
    n, d = x.shape
    ffn_dim = wd.shape[0]
    tm = min(FFN_ROW_TILE, n)
    assert n % tm == 0 and ffn_dim % FFN_COL_CHUNK == 0
    final = final_g is not None
    in_specs = [pl.BlockSpec((tm, d), lambda i: (i, 0)),
                _const_spec((1, d)), _const_spec(wgu.shape), _const_spec(wd.shape)]
    args = [x, g.reshape(1, d), wgu, wd]
    if final:
        in_specs.append(_const_spec((1, d)))
        args.append(final_g.reshape(1, d))
    return pl.pallas_call(
        functools.partial(_ffn_body, ffn_dim=ffn_dim, chunk=FFN_COL_CHUNK, final=final),
        grid=(n // tm,),
        in_specs=in_specs,
        out_specs=pl.BlockSpec((tm, d), lambda i: (i, 0)),
        out_shape=jax.ShapeDtypeStruct((n, d), F32),
        scratch_shapes=[pltpu.VMEM((tm, ffn_dim), MXU_DTYPE)],
        compiler_params=_params(1),
        name="ffn_final" if final else "ffn",
    )(*args)


def _rope_rotate(x, cos, sin_signed):
    half = 16
    lane = lax.broadcasted_iota(jnp.int32, x.shape, 1)
    up = pltpu.roll(x, LANES - half, 1)
    down = pltpu.roll(x, half, 1)
    partner = jnp.where((lane % (2 * half)) < half, up, down)
    return x * cos + partner * sin_signed


def _mix_in_body(*refs, dims, sample, tiles_per_seq, q_scale):
    (d_model, q_rank, kv_rank, rope_dim, conv_w, sgu_w, n_heads, n_groups) = dims
    it = iter(refs)
    x_ref, cos_ref, sin_ref = next(it), next(it), next(it)
    if sample:
        s0_ref, s1_ref, tab_ref, btab_ref = next(it), next(it), next(it), next(it)
    else:
        wsp_ref, bsp_ref = next(it), next(it)
    (mixn_ref, win_ref, qn_ref, wqn_ref, wqr_ref, wuk_ref, kvn_ref, cw_ref, lng_ref, lnb_ref,
     wbb_ref, wbc_ref, gb_ref) = [next(it) for _ in range(13)]
    q_out, k_out, ckv_out, kr_out, pbc_out, ga_out = [next(it) for _ in range(6)]
    if sample:
        u_out, v_out = next(it), next(it)
        ubuf, vbuf = next(it), next(it)
    else:
        nc_out = next(it)
        ubuf = next(it)

    tm = x_ref.shape[0]
    halo = SUBLANES
    j = pl.program_id(0) % tiles_per_seq

    o_q = 0
    o_kv = o_q + q_rank
    o_cb = o_kv + kv_rank
    o_cc = o_cb + conv_w
    o_cx = o_cc + conv_w
    o_su = o_cx + conv_w
    o_sv = o_su + sgu_w
    o_g = o_sv + sgu_w
    o_kr = o_g + 3 * d_model

    def proj(lo, width):
        return _dot(xn, win_ref[:, lo:lo + width])

    xn = (_rms(x_ref[...]) * mixn_ref[...]).astype(MXU_DTYPE)
    cos = cos_ref[...]
    sin = sin_ref[...]

    cq = (_rms(proj(o_q, q_rank)) * qn_ref[...]).astype(MXU_DTYPE)
    q_nope = _dot(cq, wqn_ref[...]).astype(MXU_DTYPE)
    q_rope = _dot(cq, wqr_ref[...])
    qw = kv_rank + LANES
    heads_per_block = LANES // rope_dim
    lane = lax.broadcasted_iota(jnp.int32, (tm, LANES), 1)
    rot = [_rope_rotate(q_rope[:, b * LANES:(b + 1) * LANES], cos, sin) * q_scale
           for b in range(n_heads // heads_per_block)]
    for p in range(n_heads // 2):
        ql = _dot(q_nope[:, p * LANES:(p + 1) * LANES], wuk_ref[p]) * q_scale
        for e in range(2):
            h = 2 * p + e
            lat = ql[:, e * kv_rank:(e + 1) * kv_rank].astype(q_out.dtype)
            blk, pos = divmod(h, heads_per_block)
            r = rot[blk]
            if pos:
                r = pltpu.roll(r, LANES - pos * rope_dim, 1)
            r = jnp.where(lane < rope_dim, r, 0.0).astype(q_out.dtype)
            if sample:
                q_out[:, h * qw:h * qw + kv_rank] = lat
                q_out[:, h * qw + kv_rank:(h + 1) * qw] = r
            else:
                q_out[0, h, :, :kv_rank] = lat
                q_out[0, h, :, kv_rank:] = r

    ckv = _rms(proj(o_kv, kv_rank)) * kvn_ref[...]
    ckv_out[...] = ckv
    kr = _rope_rotate(proj(o_kr, LANES), cos, sin)
    kr_out[...] = kr[:, :rope_dim]
    k_out[:, :kv_rank] = ckv.astype(k_out.dtype)
    k_out[:, kv_rank:] = kr.astype(k_out.dtype)

    u = proj(o_cc, conv_w) * proj(o_cx, conv_w)
    if sample:
        ubuf[0:halo, :] = jnp.zeros((halo, conv_w), F32)
    else:
        @pl.when(j == 0)
        def _():
            ubuf[0:halo, :] = jnp.zeros((halo, conv_w), F32)
    ubuf[halo:halo + tm, :] = u
    u1 = ubuf[halo - 1:halo - 1 + tm, :]
    u2 = ubuf[halo - 2:halo - 2 + tm, :]
    if sample:
        t_len = tab_ref.shape[0]
        t = lax.broadcasted_iota(jnp.int32, (tm, conv_w), 0) % t_len
        u1 = jnp.where(t == 0, s1_ref[...], u1)
        u2 = jnp.where(t == 0, s0_ref[...], jnp.where(t == 1, s1_ref[...], u2))
        u_out[...] = u
    else:
        ubuf[0:halo, :] = u[tm - halo:, :]

        @pl.when(j == tiles_per_seq - 1)
        def _():
            nc_out[0] = u[tm - 2:, :]
    cw = cw_ref[...]
    out_b = proj(o_cb, conv_w) * (cw[0:1] * u2 + cw[1:2] * u1 + cw[2:3] * u)

    gu = jax.nn.gelu(proj(o_su, sgu_w))
    gv = jax.nn.gelu(proj(o_sv, sgu_w))
    mu = jnp.mean(gv, axis=-1, keepdims=True)
    gc = gv - mu
    v = gc * lax.rsqrt(jnp.mean(gc * gc, axis=-1, keepdims=True) + EPS) * lng_ref[...] + lnb_ref[...]
    if sample:
        v_out[...] = v
        t_len = tab_ref.shape[0]
        vbuf[0:halo, :] = jnp.zeros((halo, sgu_w), F32)
        vbuf[halo:halo + tm, :] = v
        reps = tm // SUBLANES
        mix = btab_ref[...][None] + tab_ref[0][None] * v.reshape(reps, SUBLANES, sgu_w)
        for dlt in range(1, t_len):
            vs = vbuf[halo - dlt:halo - dlt + tm, :].reshape(reps, SUBLANES, sgu_w)
            mix = mix + tab_ref[dlt][None] * vs
        mix = mix.reshape(tm, sgu_w)
    else:
        gd = sgu_w // n_groups
        gpb = LANES // gd
        row = lax.broadcasted_iota(jnp.int32, (SGU_CHUNK, SGU_CHUNK), 0)
        col = lax.broadcasted_iota(jnp.int32, (SGU_CHUNK, SGU_CHUNK), 1)
        wt = [jnp.where(row >= col, wsp_ref[g], 0.0).astype(MXU_DTYPE) for g in range(n_groups)]
        clane = lax.broadcasted_iota(jnp.int32, (SGU_CHUNK, LANES), 1)
        vb16 = v.astype(MXU_DTYPE)
        chunks = []
        for ch in range(tm // SGU_CHUNK):
            blocks = []
            for b in range(sgu_w // LANES):
                vb = vb16[ch * SGU_CHUNK:(ch + 1) * SGU_CHUNK, b * LANES:(b + 1) * LANES]
                res = _dot(wt[b * gpb], vb)
                for e in range(1, gpb):
                    res = jnp.where(clane >= e * gd, _dot(wt[b * gpb + e], vb), res)
                blocks.append(res)
            chunks.append(jnp.concatenate(blocks, axis=-1) + bsp_ref[...])
        mix = jnp.concatenate(chunks, axis=0)
    out_c = gu * mix

    gb = gb_ref[...]
    g_a = jax.nn.sigmoid(proj(o_g, d_model) + gb[0:1])
    g_b = jax.nn.sigmoid(proj(o_g + d_model, d_model) + gb[1:2])
    g_c = jax.nn.sigmoid(proj(o_g + 2 * d_model, d_model) + gb[2:3])
    ga_out[...] = g_a
    pbc_out[...] = (g_b * _dot(out_b.astype(MXU_DTYPE), wbb_ref[...])
                    + g_c * _dot(out_c.astype(MXU_DTYPE), wbc_ref[...]))


def _mix_in(x, cos, sin, w, dims, seq_len, q_scale, sample_extra=None):
    (d_model, q_rank, kv_rank, rope_dim, conv_w, sgu_w, n_heads, n_groups) = dims
    n = x.shape[0]
    sample = sample_extra is not None
    if sample:
        tm = min(MIX_ROW_TILE, n)
        assert tm % seq_len == 0 and SUBLANES % seq_len == 0 and seq_len >= 2
        tiles_per_seq = 1
        tab_index = lambda i: (i, 0)
    else:
        tm = min(MIX_ROW_TILE, seq_len)
        assert seq_len % tm == 0 and tm % SGU_CHUNK == 0
        tiles_per_seq = seq_len // tm
        tab_index = lambda i: (i % tiles_per_seq, 0)
    assert n % tm == 0 and LANES % rope_dim == 0 and n_heads % (LANES // rope_dim) == 0
    assert n_heads % 2 == 0 and rope_dim == 32 and LANES % (sgu_w // n_groups) == 0
    qw = kv_rank + LANES
    row = lambda width: pl.BlockSpec((tm, width), lambda i: (i, 0))

    in_specs = [row(d_model), pl.BlockSpec((tm, LANES), tab_index), pl.BlockSpec((tm, LANES), tab_index)]
    args = [x, cos, sin]
    if sample:
        s0, s1, tab, btab = sample_extra
        in_specs += [row(conv_w), row(conv_w), _const_spec(tab.shape), _const_spec(btab.shape)]
        args += [s0, s1, tab, btab]
    else:
        in_specs += [_const_spec(w["wsp"].shape), _const_spec(w["bsp"].shape)]
        args += [w["wsp"], w["bsp"]]
    names = ["mix_norm", "w_in", "q_norm", "wq_nope", "wq_rope", "wuk_pair", "kv_norm", "conv_w",
             "ln_g", "ln_b", "wb_b", "wb_c", "gate_bias"]
    in_specs += [_const_spec(w[k].shape) for k in names]
    args += [w[k] for k in names]

    if sample:
        q_shape, q_spec = (n, n_heads * qw), row(n_heads * qw)
    else:
        q_shape = (n // seq_len, n_heads, seq_len, qw)
        q_spec = pl.BlockSpec((1, n_heads, tm, qw),
                              lambda i: (i // tiles_per_seq, 0, i % tiles_per_seq, 0))
    out_shape = [jax.ShapeDtypeStruct(q_shape, MXU_DTYPE),
                 jax.ShapeDtypeStruct((n, qw), MXU_DTYPE),
                 jax.ShapeDtypeStruct((n, kv_rank), F32),
                 jax.ShapeDtypeStruct((n, rope_dim), F32),
                 jax.ShapeDtypeStruct((n, d_model), F32),
                 jax.ShapeDtypeStruct((n, d_model), F32)]
    out_specs = [q_spec, row(qw), row(kv_rank), row(rope_dim), row(d_model), row(d_model)]
    scratch = [pltpu.VMEM((tm + SUBLANES, conv_w), F32)]
    if sample:
        out_shape += [jax.ShapeDtypeStruct((n, conv_w), F32), jax.ShapeDtypeStruct((n, sgu_w), F32)]
        out_specs += [row(conv_w), row(sgu_w)]
        scratch.append(pltpu.VMEM((tm + SUBLANES, sgu_w), F32))
    else:
        n_seq = n // seq_len
        out_shape.append(jax.ShapeDtypeStruct((n_seq, 2, conv_w), F32))
        out_specs.append(pl.BlockSpec((1, 2, conv_w), lambda i: (i // tiles_per_seq, 0, 0)))
    return pl.pallas_call(
        functools.partial(_mix_in_body, dims=dims, sample=sample, tiles_per_seq=tiles_per_seq,
                          q_scale=q_scale),
        grid=(n // tm,),
        in_specs=in_specs,
        out_specs=out_specs,
        out_shape=out_shape,
        scratch_shapes=scratch,
        compiler_params=_params(1),
        name="mix_in_sample" if sample else "mix_in_prompt",
    )(*args)


def _attn_prompt_body(q_ref, k_ref, o_ref, m_ref, l_ref, acc_ref, s0_ref, s1_ref, *, kv_rank):
    i = pl.program_id(1)
    heads, tq, qw = q_ref.shape[1:]
    rows = heads * tq
    kb = tq
    m_ref[...] = jnp.full(m_ref.shape, -jnp.inf, F32)
    l_ref[...] = jnp.zeros(l_ref.shape, F32)
    acc_ref[...] = jnp.zeros(acc_ref.shape, F32)

    def kblock(j):
        return k_ref[0, pl.ds(pl.multiple_of(j * kb, kb), kb), :]

    def scores(j, s_ref):
        s_ref[...] = _dot_nt(q_ref[0].reshape(rows, qw), kblock(j))

    def update(j, s_ref, diagonal):
        s = s_ref[...]
        if diagonal:
            t = lax.broadcasted_iota(jnp.int32, (heads, tq, kb), 1).reshape(rows, kb)
            c = lax.broadcasted_iota(jnp.int32, (rows, kb), 1)
            s = jnp.where(c <= t, s, -jnp.inf)
        m_prev = m_ref[...]
        m_new = jnp.maximum(m_prev, jnp.max(s, axis=-1, keepdims=True))
        alpha = jnp.exp2(m_prev - m_new)
        p = jnp.exp2(s - jnp.tile(m_new, (1, kb // LANES)))
        l_ref[...] = alpha * l_ref[...] + jnp.sum(p, axis=-1, keepdims=True)
        m_ref[...] = m_new
        acc_ref[...] = (jnp.tile(alpha, (1, kv_rank // LANES)) * acc_ref[...]
                        + _dot(p.astype(k_ref.dtype), kblock(j)[:, :kv_rank]))

    scores(0, s0_ref)

    def pair(t, carry):
        j = 2 * t
        scores(j + 1, s1_ref)
        update(j, s0_ref, False)
        scores(j + 2, s0_ref)
        update(j + 1, s1_ref, False)
        return carry

    lax.fori_loop(0, i // 2, pair, 0)

    @pl.when(i % 2 == 1)
    def _():
        scores(i, s1_ref)
        update(i - 1, s0_ref, False)
        update(i, s1_ref, True)

    @pl.when(i % 2 == 0)
    def _():
        update(i, s0_ref, True)

    o = acc_ref[...] / jnp.tile(l_ref[...], (1, kv_rank // LANES))
    o_ref[0] = o.astype(o_ref.dtype).reshape(heads, tq, kv_rank)


def _attn_prompt(q, k, kv_rank):
    b, heads, s, qw = q.shape
    tq = min(ATTN_Q_TILE, s)
    assert s % tq == 0 and tq % LANES == 0 and kv_rank % LANES == 0
    rows = tq * heads
    return pl.pallas_call(
        functools.partial(_attn_prompt_body, kv_rank=kv_rank),
        grid=(b, s // tq),
        in_specs=[pl.BlockSpec((1, heads, tq, qw), lambda bi, i: (bi, 0, i, 0)),
                  pl.BlockSpec((1, s, qw), lambda bi, i: (bi, 0, 0))],
        out_specs=pl.BlockSpec((1, heads, tq, kv_rank), lambda bi, i: (bi, 0, i, 0)),
        out_shape=jax.ShapeDtypeStruct((b, heads, s, kv_rank), MXU_DTYPE),
        scratch_shapes=[pltpu.VMEM((rows, LANES), F32), pltpu.VMEM((rows, LANES), F32),
                        pltpu.VMEM((rows, kv_rank), F32),
                        pltpu.VMEM((rows, tq), F32), pltpu.VMEM((rows, tq), F32)],
        compiler_params=_params(2),
        name="attn_prompt",
    )(q, k)


def _attn_sample_body(pt_ref, q_ref, kn_ref, lat_hbm, rope_hbm, o_ref, cbuf, rbuf, sems,
                      *, layer, heads, kv_rank):
    b = pl.program_id(0)
    nb = pl.num_programs(0)
    n_pages = cbuf.shape[1]
    page = cbuf.shape[2]
    rope_dim = rbuf.shape[1]

    def copies(bi, slot, pg):
        src = pt_ref[bi, pg]
        return (pltpu.make_async_copy(lat_hbm.at[layer, src], cbuf.at[slot, pg], sems.at[0, slot]),
                pltpu.make_async_copy(rope_hbm.at[layer, src],
                                      rbuf.at[slot, :, pl.ds(pg * page, page)], sems.at[1, slot]))

    def start_all(bi, slot):
        for pg in range(n_pages):
            for cp in copies(bi, slot, pg):
                cp.start()

    def wait_all(bi, slot):
        for pg in range(n_pages):
            for cp in copies(bi, slot, pg):
                cp.wait()

    slot = b % 2

    @pl.when(b == 0)
    def _():
        start_all(0, 0)

    @pl.when(b + 1 < nb)
    def _():
        start_all(b + 1, 1 - slot)

    wait_all(b, slot)

    q = q_ref[0]
    rows = q.shape[0]
    c = cbuf[slot].reshape(n_pages * page, kv_rank).astype(q.dtype)
    kr_t = rbuf[slot].astype(q.dtype)
    s = _dot_nt(q[:, :kv_rank], c) + _dot(q[:, kv_rank:kv_rank + rope_dim], kr_t)
    kn = kn_ref[0]
    t_new = kn.shape[0]
    s_new = _dot_nt(q, kn)
    r = lax.broadcasted_iota(jnp.int32, (rows, t_new), 0)
    cc = lax.broadcasted_iota(jnp.int32, (rows, t_new), 1)
    s_new = jnp.where(cc * heads <= r, s_new, -jnp.inf)
    m = jnp.maximum(jnp.max(s, axis=-1, keepdims=True), jnp.max(s_new, axis=-1, keepdims=True))
    p = jnp.exp2(s - m)
    p_new = jnp.exp2(s_new - m)
    l = jnp.sum(p, axis=-1, keepdims=True) + jnp.sum(p_new, axis=-1, keepdims=True)
    o = _dot(p.astype(q.dtype), c) + _dot(p_new.astype(q.dtype), kn[:, :kv_rank])
    o_ref[0] = (o / l).astype(o_ref.dtype)


def _attn_sample(page_table, q, k_new, cache_lat, cache_rope_t, layer, heads, kv_rank):
    bs, rows, qw = q.shape
    t_new = k_new.shape[1]
    n_pages = page_table.shape[1]
    page = cache_lat.shape[2]
    rope_dim = cache_rope_t.shape[2]
    grid_spec = pltpu.PrefetchScalarGridSpec(
        num_scalar_prefetch=1,
        grid=(bs,),
        in_specs=[pl.BlockSpec((1, rows, qw), lambda b, pt: (b, 0, 0)),
                  pl.BlockSpec((1, t_new, qw), lambda b, pt: (b, 0, 0)),
                  pl.BlockSpec(memory_space=pl.ANY),
                  pl.BlockSpec(memory_space=pl.ANY)],
        out_specs=pl.BlockSpec((1, rows, kv_rank), lambda b, pt: (b, 0, 0)),
        scratch_shapes=[pltpu.VMEM((2, n_pages, page, kv_rank), F32),
                        pltpu.VMEM((2, rope_dim, n_pages * page), F32),
                        pltpu.SemaphoreType.DMA((2, 2))])
    return pl.pallas_call(
        functools.partial(_attn_sample_body, layer=layer, heads=heads, kv_rank=kv_rank),
        grid_spec=grid_spec,
        out_shape=jax.ShapeDtypeStruct((bs, rows, kv_rank), MXU_DTYPE),
        compiler_params=_params(1),
        name="attn_sample",
    )(page_table, q, k_new, cache_lat, cache_rope_t)


def _mix_out_body(o_ref, x_ref, pbc_ref, ga_ref, wuv_ref, wba_ref, wout_ref, out_ref, *, head_major):
    n_pairs = wuv_ref.shape[0]
    kv = wuv_ref.shape[1] // 2
    parts = []
    for p in range(n_pairs):
        if head_major:
            parts.append(_dot(o_ref[0, 2 * p], wuv_ref[p, :kv, :])
                         + _dot(o_ref[0, 2 * p + 1], wuv_ref[p, kv:, :]))
        else:
            parts.append(_dot(o_ref[:, 2 * p * kv:(2 * p + 2) * kv], wuv_ref[p]))
    out_a = jnp.concatenate(parts, axis=-1).astype(MXU_DTYPE)
    merged = pbc_ref[...] + ga_ref[...] * _dot(out_a, wba_ref[...])
    out_ref[...] = x_ref[...] + _dot(merged.astype(MXU_DTYPE), wout_ref[...])


def _mix_out(o_lat, x, pbc, ga, wuv_pair, wb_a, w_out):
    n, d = x.shape
    head_major = o_lat.ndim == 4
    row = lambda width: pl.BlockSpec((tm, width), lambda i: (i, 0))
    if head_major:
        _, heads, s, kv = o_lat.shape
        tm = min(OUT_ROW_TILE, s)
        assert s % tm == 0
        tps = s // tm
        o_spec = pl.BlockSpec((1, heads, tm, kv), lambda i: (i // tps, 0, i % tps, 0))
    else:
        tm = min(OUT_ROW_TILE, n)
        o_spec = row(o_lat.shape[1])
    assert n % tm == 0
    return pl.pallas_call(
        functools.partial(_mix_out_body, head_major=head_major),
        grid=(n // tm,),
        in_specs=[o_spec, row(d), row(d), row(d),
                  _const_spec(wuv_pair.shape), _const_spec(wb_a.shape), _const_spec(w_out.shape)],
        out_specs=row(d),
        out_shape=jax.ShapeDtypeStruct((n, d), F32),
        compiler_params=_params(1),
        name="mix_out",
    )(o_lat, x, pbc, ga, wuv_pair, wb_a, w_out)


def _pair_block_diag(t):
    l, h, r, c = t.shape
    t = t.reshape(l, h // 2, 2, r, c)
    z = jnp.zeros((l, h // 2, r, c), t.dtype)
    top = jnp.concatenate([t[:, :, 0], z], axis=-1)
    bot = jnp.concatenate([z, t[:, :, 1]], axis=-1)
    return jnp.concatenate([top, bot], axis=-2)


def _rope_tables(pos, rope_dim):
    half = rope_dim // 2
    inv = ROPE_THETA ** (-jnp.arange(half, dtype=F32) / half)
    ang = pos.astype(F32)[:, None] * inv[None, :]
    cos, sin = jnp.cos(ang), jnp.sin(ang)
    reps = LANES // rope_dim
    return (jnp.tile(jnp.concatenate([cos, cos], axis=-1), (1, reps)),
            jnp.tile(jnp.concatenate([-sin, sin], axis=-1), (1, reps)))


def kernel(x_prompt, x_sample, cache_kv_latent, cache_k_rope, state_conv, page_table, ffn1_norm, ffn1_w_gate_up, ffn1_w_down, mix_norm, w_in, gate_bias, q_norm, w_uq, kv_norm, w_uk, w_uv, conv_w, sgu_ln_g, sgu_ln_b, sgu_w_spatial, sgu_b_spatial, w_branch, w_out, ffn2_norm, ffn2_w_gate_up, ffn2_w_down, final_norm):
    bp, s, d_model = x_prompt.shape
    bs, t_new, _ = x_sample.shape
    depth = w_in.shape[0]
    q_rank = q_norm.shape[-1]
    kv_rank = kv_norm.shape[-1]
    rope_dim = cache_k_rope.shape[-1]
    n_heads, nope_dim = w_uk.shape[2], w_uk.shape[3]
    v_dim = w_uv.shape[3]
    cw_dim = conv_w.shape[-1]
    conv_k = conv_w.shape[1]
    sgu_w = sgu_ln_g.shape[-1]
    n_groups = sgu_w_spatial.shape[1]
    group_dim = sgu_w // n_groups
    n_branch = w_branch.shape[1]
    past_len = page_table.shape[1] * cache_kv_latent.shape[2]
    assert conv_k == 3 and n_branch == 3 and s >= SGU_CHUNK and t_new < SGU_CHUNK
    assert w_uq.shape[3] == nope_dim + rope_dim and 2 * nope_dim == LANES and 2 * v_dim == LANES
    dims = (d_model, q_rank, kv_rank, rope_dim, cw_dim, sgu_w, n_heads, n_groups)
    cdt = MXU_DTYPE

    o_kr = q_rank + kv_rank
    w_in_p = jnp.concatenate(
        [w_in[..., :o_kr], w_in[..., o_kr + rope_dim:], w_in[..., o_kr:o_kr + rope_dim],
         jnp.zeros((depth, d_model, LANES - rope_dim), w_in.dtype)], axis=-1).astype(cdt)
    q_scale = (nope_dim + rope_dim) ** -0.5 * math.log2(math.e)
    wq_nope = w_uq[..., :nope_dim].reshape(depth, q_rank, n_heads * nope_dim).astype(cdt)
    wq_rope = w_uq[..., nope_dim:].reshape(depth, q_rank, n_heads * rope_dim).astype(cdt)
    wuk_pair = _pair_block_diag(w_uk.transpose(0, 2, 3, 1)).astype(cdt)
    wuv_pair = _pair_block_diag(w_uv.transpose(0, 2, 1, 3)).astype(cdt)
    wb = w_branch.astype(cdt)
    w_out_c = w_out.astype(cdt)
    f1gu, f1d = ffn1_w_gate_up.astype(cdt), ffn1_w_down.astype(cdt)
    f2gu, f2d = ffn2_w_gate_up.astype(cdt), ffn2_w_down.astype(cdt)

    bsp = jnp.repeat(sgu_b_spatial[:, :, :SGU_CHUNK].transpose(0, 2, 1), group_dim, axis=-1)
    w4 = jnp.tril(sgu_w_spatial[:, :, :t_new, :t_new])
    tabs = []
    for dlt in range(t_new):
        diag = jnp.diagonal(w4, offset=-dlt, axis1=2, axis2=3)
        diag = jnp.pad(diag, ((0, 0), (0, 0), (dlt, 0)))
        tabs.append(jnp.repeat(diag.transpose(0, 2, 1), group_dim, axis=-1))
    reps8 = SUBLANES // t_new
    tab_s = jnp.tile(jnp.stack(tabs, axis=1), (1, 1, reps8, 1))
    btab_s = jnp.tile(jnp.repeat(sgu_b_spatial[:, :, :t_new].transpose(0, 2, 1), group_dim, axis=-1),
                      (1, reps8, 1))

    cos_p, sin_p = _rope_tables(jnp.arange(s), rope_dim)
    cos_s, sin_s = _rope_tables(past_len + jnp.arange(t_new), rope_dim)
    cos_s, sin_s = jnp.tile(cos_s, (bs, 1)), jnp.tile(sin_s, (bs, 1))

    cache_rope_t = jnp.swapaxes(cache_k_rope, 2, 3)

    xp = x_prompt.reshape(bp * s, d_model)
    xs = x_sample.reshape(bs * t_new, d_model)
    qw = kv_rank + LANES
    lat_p, kr_p, conv_p, lat_s, kr_s, conv_s, v_s = [], [], [], [], [], [], []
    for l in range(depth):
        w = {"mix_norm": mix_norm[l].reshape(1, -1), "w_in": w_in_p[l], "q_norm": q_norm[l].reshape(1, -1),
             "wq_nope": wq_nope[l], "wq_rope": wq_rope[l], "wuk_pair": wuk_pair[l],
             "kv_norm": kv_norm[l].reshape(1, -1), "conv_w": conv_w[l],
             "ln_g": sgu_ln_g[l].reshape(1, -1), "ln_b": sgu_ln_b[l].reshape(1, -1),
             "wb_b": wb[l, 1], "wb_c": wb[l, 2], "gate_bias": gate_bias[l],
             "wsp": sgu_w_spatial[l][:, :SGU_CHUNK, :SGU_CHUNK], "bsp": bsp[l]}
        last = l == depth - 1

        xp = _ffn(xp, ffn1_norm[l], f1gu[l], f1d[l])
        q, kk, ckv, kr, pbc, ga, nconv = _mix_in(xp, cos_p, sin_p, w, dims, s, q_scale)
        o = _attn_prompt(q, kk.reshape(bp, s, qw), kv_rank)
        xp = _mix_out(o, xp, pbc, ga, wuv_pair[l], wb[l, 0], w_out_c[l])
        xp = _ffn(xp, ffn2_norm[l], f2gu[l], f2d[l], final_norm if last else None)
        lat_p.append(ckv.reshape(bp, s, kv_rank))
        kr_p.append(kr.reshape(bp, s, rope_dim))
        conv_p.append(nconv)

        st = state_conv[l]
        s0 = jnp.repeat(st[:, 0], t_new, axis=0)
        s1 = jnp.repeat(st[:, 1], t_new, axis=0)
        xs = _ffn(xs, ffn1_norm[l], f1gu[l], f1d[l])
        q, kk, ckv, kr, pbc, ga, u, v = _mix_in(xs, cos_s, sin_s, w, dims, t_new, q_scale,
                                                sample_extra=(s0, s1, tab_s[l], btab_s[l]))
        o = _attn_sample(page_table, q.reshape(bs, t_new * n_heads, qw), kk.reshape(bs, t_new, qw),
                         cache_kv_latent, cache_rope_t, l, n_heads, kv_rank)
        xs = _mix_out(o.reshape(bs * t_new, n_heads * kv_rank), xs, pbc, ga, wuv_pair[l], wb[l, 0], w_out_c[l])
        xs = _ffn(xs, ffn2_norm[l], f2gu[l], f2d[l], final_norm if last else None)
        lat_s.append(ckv.reshape(bs, t_new, kv_rank))
        kr_s.append(kr.reshape(bs, t_new, rope_dim))
        conv_s.append(u.reshape(bs, t_new, cw_dim)[:, t_new - (conv_k - 1):])
        v_s.append(v.reshape(bs, t_new, sgu_w))

    return (xp.reshape(bp, s, d_model), xs.reshape(bs, t_new, d_model),
            jnp.stack(lat_p), jnp.stack(kr_p), jnp.stack(conv_p),
            jnp.stack(lat_s), jnp.stack(kr_s), jnp.stack(conv_s), jnp.stack(v_s))
```

```python
import functools
import math

import jax
import jax.numpy as jnp
from jax import lax
from jax.experimental import pallas as pl
from jax.experimental.pallas import tpu as pltpu

F32 = jnp.float32
MXU_DTYPE = jnp.bfloat16

EPS = 1e-6
ROPE_THETA = 10000.0
SGU_CHUNK = 128
LANES = 128
SUBLANES = 8
VMEM_LIMIT_BYTES = 56 * 1024 * 1024

FFN_ROW_TILE = 512
FFN_COL_CHUNK = 256
MIX_ROW_TILE = 512
OUT_ROW_TILE = 512
ATTN_Q_TILE = 256


def _dot(a, b):
    return jnp.dot(a, b, preferred_element_type=F32)


def _dot_nt(a, b):
    return lax.dot_general(a, b, (((1,), (1,)), ((), ())), preferred_element_type=F32)


def _rms(x):
    return x * lax.rsqrt(jnp.mean(x * x, axis=-1, keepdims=True) + EPS)


def _const_spec(shape):
    zeros = (0,) * len(shape)
    return pl.BlockSpec(shape, lambda *_: zeros, pipeline_mode=pl.Buffered(1))


def _params(n_grid_axes):
    return pltpu.CompilerParams(
        dimension_semantics=("arbitrary",) * n_grid_axes,
        vmem_limit_bytes=VMEM_LIMIT_BYTES)


def _ffn_body(*refs, ffn_dim, chunk, final):
    if final:
        x_ref, g_ref, wgu_ref, wd_ref, fg_ref, o_ref, act_ref = refs
    else:
        x_ref, g_ref, wgu_ref, wd_ref, o_ref, act_ref = refs
    x = x_ref[...]
    xn = (_rms(x) * g_ref[...]).astype(MXU_DTYPE)
    for c in range(ffn_dim // chunk):
        a = _dot(xn, wgu_ref[:, c * chunk:(c + 1) * chunk])
        b = _dot(xn, wgu_ref[:, ffn_dim + c * chunk:ffn_dim + (c + 1) * chunk])
        act_ref[:, c * chunk:(c + 1) * chunk] = (a * jax.nn.sigmoid(a) * b).astype(MXU_DTYPE)
    out = x + 0.5 * _dot(act_ref[...], wd_ref[...])
    if final:
        out = _rms(out) * fg_ref[...]
    o_ref[...] = out


def _ffn(x, g, wgu, wd, final_g=None):
    n, d = x.shape
    ffn_dim = wd.shape[0]
    tm = min(FFN_ROW_TILE, n)
    assert n % tm == 0 and ffn_dim % FFN_COL_CHUNK == 0
    final = final_g is not None
    in_specs = [pl.BlockSpec((tm, d), lambda i: (i, 0)),
                _const_spec((1, d)), _const_spec(wgu.shape), _const_spec(wd.shape)]
    args = [x, g.reshape(1, d), wgu, wd]
    if final:
        in_specs.append(_const_spec((1, d)))
        args.append(final_g.reshape(1, d))
    return pl.pallas_call(
        functools.partial(_ffn_body, ffn_dim=ffn_dim, chunk=FFN_COL_CHUNK, final=final),
        grid=(n // tm,),
        in_specs=in_specs,
        out_specs=pl.BlockSpec((tm, d), lambda i: (i, 0)),
        out_shape=jax.ShapeDtypeStruct((n, d), F32),
        scratch_shapes=[pltpu.VMEM((tm, ffn_dim), MXU_DTYPE)],
        compiler_params=_params(1),
        name="ffn_final" if final else "ffn",
    )(*args)


def _rope_rotate(x, cos, sin_signed):
    half = 16
    lane = lax.broadcasted_iota(jnp.int32, x.shape, 1)
    up = pltpu.roll(x, LANES - half, 1)
    down = pltpu.roll(x, half, 1)
    partner = jnp.where((lane % (2 * half)) < half, up, down)
    return x * cos + partner * sin_signed


def _mix_in_body(*refs, dims, sample, tiles_per_seq, q_scale):
    (d_model, q_rank, kv_rank, rope_dim, conv_w, sgu_w, n_heads, n_groups) = dims
    it = iter(refs)
    x_ref, cos_ref, sin_ref = next(it), next(it), next(it)
    if sample:
        s0_ref, s1_ref, tab_ref, btab_ref = next(it), next(it), next(it), next(it)
    else:
        wsp_ref, bsp_ref = next(it), next(it)
    (mixn_ref, win_ref, qn_ref, wqn_ref, wqr_ref, wuk_ref, kvn_ref, cw_ref, lng_ref, lnb_ref,
     wbb_ref, wbc_ref, gb_ref) = [next(it) for _ in range(13)]
    q_out, k_out, ckv_out, kr_out, pbc_out, ga_out = [next(it) for _ in range(6)]
    if sample:
        u_out, v_out = next(it), next(it)
        ubuf, vbuf = next(it), next(it)
    else:
        nc_out = next(it)
        ubuf = next(it)

    tm = x_ref.shape[0]
    halo = SUBLANES
    j = pl.program_id(0) % tiles_per_seq

    o_q = 0
    o_kv = o_q + q_rank
    o_cb = o_kv + kv_rank
    o_cc = o_cb + conv_w
    o_cx = o_cc + conv_w
    o_su = o_cx + conv_w
    o_sv = o_su + sgu_w
    o_g = o_sv + sgu_w
    o_kr = o_g + 3 * d_model

    def proj(lo, width):
        return _dot(xn, win_ref[:, lo:lo + width])

    xn = (_rms(x_ref[...]) * mixn_ref[...]).astype(MXU_DTYPE)
    cos = cos_ref[...]
    sin = sin_ref[...]

    cq = (_rms(proj(o_q, q_rank)) * qn_ref[...]).astype(MXU_DTYPE)
    q_nope = _dot(cq, wqn_ref[...]).astype(MXU_DTYPE)
    q_rope = _dot(cq, wqr_ref[...])
    qw = kv_rank + LANES
    heads_per_block = LANES // rope_dim
    lane = lax.broadcasted_iota(jnp.int32, (tm, LANES), 1)
    rot = [_rope_rotate(q_rope[:, b * LANES:(b + 1) * LANES], cos, sin) * q_scale
           for b in range(n_heads // heads_per_block)]
    for p in range(n_heads // 2):
        ql = _dot(q_nope[:, p * LANES:(p + 1) * LANES], wuk_ref[p]) * q_scale
        for e in range(2):
            h = 2 * p + e
            lat = ql[:, e * kv_rank:(e + 1) * kv_rank].astype(q_out.dtype)
            blk, pos = divmod(h, heads_per_block)
            r = rot[blk]
            if pos:
                r = pltpu.roll(r, LANES - pos * rope_dim, 1)
            r = jnp.where(lane < rope_dim, r, 0.0).astype(q_out.dtype)
            if sample:
                q_out[:, h * qw:h * qw + kv_rank] = lat
                q_out[:, h * qw + kv_rank:(h + 1) * qw] = r
            else:
                q_out[0, h, :, :kv_rank] = lat
                q_out[0, h, :, kv_rank:] = r

    ckv = _rms(proj(o_kv, kv_rank)) * kvn_ref[...]
    ckv_out[...] = ckv
    kr = _rope_rotate(proj(o_kr, LANES), cos, sin)
    kr_out[...] = kr[:, :rope_dim]
    k_out[:, :kv_rank] = ckv.astype(k_out.dtype)
    k_out[:, kv_rank:] = kr.astype(k_out.dtype)

    u = proj(o_cc, conv_w) * proj(o_cx, conv_w)
    if sample:
        ubuf[0:halo, :] = jnp.zeros((halo, conv_w), F32)
    else:
        @pl.when(j == 0)
        def _():
            ubuf[0:halo, :] = jnp.zeros((halo, conv_w), F32)
    ubuf[halo:halo + tm, :] = u
    u1 = ubuf[halo - 1:halo - 1 + tm, :]
    u2 = ubuf[halo - 2:halo - 2 + tm, :]
    if sample:
        t_len = tab_ref.shape[0]
        t = lax.broadcasted_iota(jnp.int32, (tm, conv_w), 0) % t_len
        u1 = jnp.where(t == 0, s1_ref[...], u1)
        u2 = jnp.where(t == 0, s0_ref[...], jnp.where(t == 1, s1_ref[...], u2))
        u_out[...] = u
    else:
        ubuf[0:halo, :] = u[tm - halo:, :]

        @pl.when(j == tiles_per_seq - 1)
        def _():
            nc_out[0] = u[tm - 2:, :]
    cw = cw_ref[...]
    out_b = proj(o_cb, conv_w) * (cw[0:1] * u2 + cw[1:2] * u1 + cw[2:3] * u)

    gu = jax.nn.gelu(proj(o_su, sgu_w))
    gv = jax.nn.gelu(proj(o_sv, sgu_w))
    mu = jnp.mean(gv, axis=-1, keepdims=True)
    gc = gv - mu
    v = gc * lax.rsqrt(jnp.mean(gc * gc, axis=-1, keepdims=True) + EPS) * lng_ref[...] + lnb_ref[...]
    if sample:
        v_out[...] = v
        t_len = tab_ref.shape[0]
        vbuf[0:halo, :] = jnp.zeros((halo, sgu_w), F32)
        vbuf[halo:halo + tm, :] = v
        reps = tm // SUBLANES
        mix = btab_ref[...][None] + tab_ref[0][None] * v.reshape(reps, SUBLANES, sgu_w)
        for dlt in range(1, t_len):
            vs = vbuf[halo - dlt:halo - dlt + tm, :].reshape(reps, SUBLANES, sgu_w)
            mix = mix + tab_ref[dlt][None] * vs
        mix = mix.reshape(tm, sgu_w)
    else:
        gd = sgu_w // n_groups
        gpb = LANES // gd
        row = lax.broadcasted_iota(jnp.int32, (SGU_CHUNK, SGU_CHUNK), 0)
        col = lax.broadcasted_iota(jnp.int32, (SGU_CHUNK, SGU_CHUNK), 1)
        wt = [jnp.where(row >= col, wsp_ref[g], 0.0).astype(MXU_DTYPE) for g in range(n_groups)]
        clane = lax.broadcasted_iota(jnp.int32, (SGU_CHUNK, LANES), 1)
        vb16 = v.astype(MXU_DTYPE)
        chunks = []
        for ch in range(tm // SGU_CHUNK):
            blocks = []
            for b in range(sgu_w // LANES):
                vb = vb16[ch * SGU_CHUNK:(ch + 1) * SGU_CHUNK, b * LANES:(b + 1) * LANES]
                res = _dot(wt[b * gpb], vb)
                for e in range(1, gpb):
                    res = jnp.where(clane >= e * gd, _dot(wt[b * gpb + e], vb), res)
                blocks.append(res)
            chunks.append(jnp.concatenate(blocks, axis=-1) + bsp_ref[...])
        mix = jnp.concatenate(chunks, axis=0)
    out_c = gu * mix

    gb = gb_ref[...]
    g_a = jax.nn.sigmoid(proj(o_g, d_model) + gb[0:1])
    g_b = jax.nn.sigmoid(proj(o_g + d_model, d_model) + gb[1:2])
    g_c = jax.nn.sigmoid(proj(o_g + 2 * d_model, d_model) + gb[2:3])
    ga_out[...] = g_a.astype(ga_out.dtype)
    pbc_out[...] = (g_b * _dot(out_b.astype(MXU_DTYPE), wbb_ref[...])
                    + g_c * _dot(out_c.astype(MXU_DTYPE), wbc_ref[...])).astype(pbc_out.dtype)


def _mix_in(x, cos, sin, w, dims, seq_len, q_scale, sample_extra=None):
    (d_model, q_rank, kv_rank, rope_dim, conv_w, sgu_w, n_heads, n_groups) = dims
    n = x.shape[0]
    sample = sample_extra is not None
    if sample:
        tm = min(MIX_ROW_TILE, n)
        assert tm % seq_len == 0 and SUBLANES % seq_len == 0 and seq_len >= 2
        tiles_per_seq = 1
        tab_index = lambda i: (i, 0)
    else:
        tm = min(MIX_ROW_TILE, seq_len)
        assert seq_len % tm == 0 and tm % SGU_CHUNK == 0
        tiles_per_seq = seq_len // tm
        tab_index = lambda i: (i % tiles_per_seq, 0)
    assert n % tm == 0 and LANES % rope_dim == 0 and n_heads % (LANES // rope_dim) == 0
    assert n_heads % 2 == 0 and rope_dim == 32 and LANES % (sgu_w // n_groups) == 0
    qw = kv_rank + LANES
    row = lambda width: pl.BlockSpec((tm, width), lambda i: (i, 0))

    in_specs = [row(d_model), pl.BlockSpec((tm, LANES), tab_index), pl.BlockSpec((tm, LANES), tab_index)]
    args = [x, cos, sin]
    if sample:
        s0, s1, tab, btab = sample_extra
        in_specs += [row(conv_w), row(conv_w), _const_spec(tab.shape), _const_spec(btab.shape)]
        args += [s0, s1, tab, btab]
    else:
        in_specs += [_const_spec(w["wsp"].shape), _const_spec(w["bsp"].shape)]
        args += [w["wsp"], w["bsp"]]
    names = ["mix_norm", "w_in", "q_norm", "wq_nope", "wq_rope", "wuk_pair", "kv_norm", "conv_w",
             "ln_g", "ln_b", "wb_b", "wb_c", "gate_bias"]
    in_specs += [_const_spec(w[k].shape) for k in names]
    args += [w[k] for k in names]

    if sample:
        q_shape, q_spec = (n, n_heads * qw), row(n_heads * qw)
    else:
        q_shape = (n // seq_len, n_heads, seq_len, qw)
        q_spec = pl.BlockSpec((1, n_heads, tm, qw),
                              lambda i: (i // tiles_per_seq, 0, i % tiles_per_seq, 0))
    out_shape = [jax.ShapeDtypeStruct(q_shape, MXU_DTYPE),
                 jax.ShapeDtypeStruct((n, qw), MXU_DTYPE),
                 jax.ShapeDtypeStruct((n, kv_rank), F32),
                 jax.ShapeDtypeStruct((n, rope_dim), F32),
                 jax.ShapeDtypeStruct((n, d_model), MXU_DTYPE),
                 jax.ShapeDtypeStruct((n, d_model), MXU_DTYPE)]
    out_specs = [q_spec, row(qw), row(kv_rank), row(rope_dim), row(d_model), row(d_model)]
    scratch = [pltpu.VMEM((tm + SUBLANES, conv_w), F32)]
    if sample:
        out_shape += [jax.ShapeDtypeStruct((n, conv_w), F32), jax.ShapeDtypeStruct((n, sgu_w), F32)]
        out_specs += [row(conv_w), row(sgu_w)]
        scratch.append(pltpu.VMEM((tm + SUBLANES, sgu_w), F32))
    else:
        n_seq = n // seq_len
        out_shape.append(jax.ShapeDtypeStruct((n_seq, 2, conv_w), F32))
        out_specs.append(pl.BlockSpec((1, 2, conv_w), lambda i: (i // tiles_per_seq, 0, 0)))
    return pl.pallas_call(
        functools.partial(_mix_in_body, dims=dims, sample=sample, tiles_per_seq=tiles_per_seq,
                          q_scale=q_scale),
        grid=(n // tm,),
        in_specs=in_specs,
        out_specs=out_specs,
        out_shape=out_shape,
        scratch_shapes=scratch,
        compiler_params=_params(1),
        name="mix_in_sample" if sample else "mix_in_prompt",
    )(*args)


def _attn_prompt_body(q_ref, k_ref, o_ref, m_ref, l_ref, acc_ref, s0_ref, s1_ref, *, kv_rank):
    i = pl.program_id(1)
    heads, tq, qw = q_ref.shape[1:]
    rows = heads * tq
    kb = tq
    m_ref[...] = jnp.full(m_ref.shape, -jnp.inf, F32)
    l_ref[...] = jnp.zeros(l_ref.shape, F32)
    acc_ref[...] = jnp.zeros(acc_ref.shape, F32)

    def kblock(j):
        return k_ref[0, pl.ds(pl.multiple_of(j * kb, kb), kb), :]

    def scores(j, s_ref):
        s_ref[...] = _dot_nt(q_ref[0].reshape(rows, qw), kblock(j))

    def update(j, s_ref, diagonal):
        s = s_ref[...]
        if diagonal:
            t = lax.broadcasted_iota(jnp.int32, (heads, tq, kb), 1).reshape(rows, kb)
            c = lax.broadcasted_iota(jnp.int32, (rows, kb), 1)
            s = jnp.where(c <= t, s, -jnp.inf)
        m_prev = m_ref[...]
        m_new = jnp.maximum(m_prev, jnp.max(s, axis=-1, keepdims=True))
        alpha = jnp.exp2(m_prev - m_new)
        p = jnp.exp2(s - jnp.tile(m_new, (1, kb // LANES)))
        p_lanes = p[:, :LANES]
        for b in range(1, kb // LANES):
            p_lanes = p_lanes + p[:, b * LANES:(b + 1) * LANES]
        l_ref[...] = alpha * l_ref[...] + p_lanes
        m_ref[...] = m_new
        acc_ref[...] = (jnp.tile(alpha, (1, kv_rank // LANES)) * acc_ref[...]
                        + _dot(p.astype(k_ref.dtype), kblock(j)[:, :kv_rank]))

    scores(0, s0_ref)

    def pair(t, carry):
        j = 2 * t
        scores(j + 1, s1_ref)
        update(j, s0_ref, False)
        scores(j + 2, s0_ref)
        update(j + 1, s1_ref, False)
        return carry

    lax.fori_loop(0, i // 2, pair, 0)

    @pl.when(i % 2 == 1)
    def _():
        scores(i, s1_ref)
        update(i - 1, s0_ref, False)
        update(i, s1_ref, True)

    @pl.when(i % 2 == 0)
    def _():
        update(i, s0_ref, True)

    o = acc_ref[...] / jnp.sum(l_ref[...], axis=-1, keepdims=True)
    o_ref[0] = o.astype(o_ref.dtype).reshape(heads, tq, kv_rank)


def _attn_prompt(q, k, kv_rank):
    b, heads, s, qw = q.shape
    tq = min(ATTN_Q_TILE, s)
    assert s % tq == 0 and tq % LANES == 0 and kv_rank % LANES == 0
    rows = tq * heads
    return pl.pallas_call(
        functools.partial(_attn_prompt_body, kv_rank=kv_rank),
        grid=(b, s // tq),
        in_specs=[pl.BlockSpec((1, heads, tq, qw), lambda bi, i: (bi, 0, i, 0)),
                  pl.BlockSpec((1, s, qw), lambda bi, i: (bi, 0, 0))],
        out_specs=pl.BlockSpec((1, heads, tq, kv_rank), lambda bi, i: (bi, 0, i, 0)),
        out_shape=jax.ShapeDtypeStruct((b, heads, s, kv_rank), MXU_DTYPE),
        scratch_shapes=[pltpu.VMEM((rows, LANES), F32), pltpu.VMEM((rows, LANES), F32),
                        pltpu.VMEM((rows, kv_rank), F32),
                        pltpu.VMEM((rows, tq), F32), pltpu.VMEM((rows, tq), F32)],
        compiler_params=_params(2),
        name="attn_prompt",
    )(q, k)


def _attn_sample_body(pt_ref, q_ref, kn_ref, lat_hbm, rope_hbm, o_ref, cbuf, rbuf, sems,
                      *, layer, heads, kv_rank):
    b = pl.program_id(0)
    nb = pl.num_programs(0)
    n_pages = cbuf.shape[1]
    page = cbuf.shape[2]
    rope_dim = rbuf.shape[1]

    def copies(bi, slot, pg):
        src = pt_ref[bi, pg]
        return (pltpu.make_async_copy(lat_hbm.at[layer, src], cbuf.at[slot, pg], sems.at[0, slot]),
                pltpu.make_async_copy(rope_hbm.at[layer, src],
                                      rbuf.at[slot, :, pl.ds(pg * page, page)], sems.at[1, slot]))

    def start_all(bi, slot):
        for pg in range(n_pages):
            for cp in copies(bi, slot, pg):
                cp.start()

    def wait_all(bi, slot):
        for pg in range(n_pages):
            for cp in copies(bi, slot, pg):
                cp.wait()

    slot = b % 2

    @pl.when(b == 0)
    def _():
        start_all(0, 0)

    @pl.when(b + 1 < nb)
    def _():
        start_all(b + 1, 1 - slot)

    wait_all(b, slot)

    q = q_ref[0]
    rows = q.shape[0]
    c = cbuf[slot].reshape(n_pages * page, kv_rank).astype(q.dtype)
    kr_t = rbuf[slot].astype(q.dtype)
    s = _dot_nt(q[:, :kv_rank], c) + _dot(q[:, kv_rank:kv_rank + rope_dim], kr_t)
    kn = kn_ref[0]
    t_new = kn.shape[0]
    s_new = _dot_nt(q, kn)
    r = lax.broadcasted_iota(jnp.int32, (rows, t_new), 0)
    cc = lax.broadcasted_iota(jnp.int32, (rows, t_new), 1)
    s_new = jnp.where(cc * heads <= r, s_new, -jnp.inf)
    m = jnp.maximum(jnp.max(s, axis=-1, keepdims=True), jnp.max(s_new, axis=-1, keepdims=True))
    p = jnp.exp2(s - m)
    p_new = jnp.exp2(s_new - m)
    l = jnp.sum(p, axis=-1, keepdims=True) + jnp.sum(p_new, axis=-1, keepdims=True)
    o = _dot(p.astype(q.dtype), c) + _dot(p_new.astype(q.dtype), kn[:, :kv_rank])
    o_ref[0] = (o / l).astype(o_ref.dtype)


def _attn_sample(page_table, q, k_new, cache_lat, cache_rope_t, layer, heads, kv_rank):
    bs, rows, qw = q.shape
    t_new = k_new.shape[1]
    n_pages = page_table.shape[1]
    page = cache_lat.shape[2]
    rope_dim = cache_rope_t.shape[2]
    grid_spec = pltpu.PrefetchScalarGridSpec(
        num_scalar_prefetch=1,
        grid=(bs,),
        in_specs=[pl.BlockSpec((1, rows, qw), lambda b, pt: (b, 0, 0)),
                  pl.BlockSpec((1, t_new, qw), lambda b, pt: (b, 0, 0)),
                  pl.BlockSpec(memory_space=pl.ANY),
                  pl.BlockSpec(memory_space=pl.ANY)],
        out_specs=pl.BlockSpec((1, rows, kv_rank), lambda b, pt: (b, 0, 0)),
        scratch_shapes=[pltpu.VMEM((2, n_pages, page, kv_rank), F32),
                        pltpu.VMEM((2, rope_dim, n_pages * page), F32),
                        pltpu.SemaphoreType.DMA((2, 2))])
    return pl.pallas_call(
        functools.partial(_attn_sample_body, layer=layer, heads=heads, kv_rank=kv_rank),
        grid_spec=grid_spec,
        out_shape=jax.ShapeDtypeStruct((bs, rows, kv_rank), MXU_DTYPE),
        compiler_params=_params(1),
        name="attn_sample",
    )(page_table, q, k_new, cache_lat, cache_rope_t)


def _mix_out_body(o_ref, x_ref, pbc_ref, ga_ref, wuv_ref, wba_ref, wout_ref, out_ref, *, head_major):
    n_pairs = wuv_ref.shape[0]
    kv = wuv_ref.shape[1] // 2
    parts = []
    for p in range(n_pairs):
        if head_major:
            parts.append(_dot(o_ref[0, 2 * p], wuv_ref[p, :kv, :])
                         + _dot(o_ref[0, 2 * p + 1], wuv_ref[p, kv:, :]))
        else:
            parts.append(_dot(o_ref[:, 2 * p * kv:(2 * p + 2) * kv], wuv_ref[p]))
    out_a = jnp.concatenate(parts, axis=-1).astype(MXU_DTYPE)
    merged = pbc_ref[...].astype(F32) + ga_ref[...].astype(F32) * _dot(out_a, wba_ref[...])
    out_ref[...] = x_ref[...] + _dot(merged.astype(MXU_DTYPE), wout_ref[...])


def _mix_out(o_lat, x, pbc, ga, wuv_pair, wb_a, w_out):
    n, d = x.shape
    head_major = o_lat.ndim == 4
    row = lambda width: pl.BlockSpec((tm, width), lambda i: (i, 0))
    if head_major:
        _, heads, s, kv = o_lat.shape
        tm = min(OUT_ROW_TILE, s)
        assert s % tm == 0
        tps = s // tm
        o_spec = pl.BlockSpec((1, heads, tm, kv), lambda i: (i // tps, 0, i % tps, 0))
    else:
        tm = min(OUT_ROW_TILE, n)
        o_spec = row(o_lat.shape[1])
    assert n % tm == 0
    return pl.pallas_call(
        functools.partial(_mix_out_body, head_major=head_major),
        grid=(n // tm,),
        in_specs=[o_spec, row(d), row(d), row(d),
                  _const_spec(wuv_pair.shape), _const_spec(wb_a.shape), _const_spec(w_out.shape)],
        out_specs=row(d),
        out_shape=jax.ShapeDtypeStruct((n, d), F32),
        compiler_params=_params(1),
        name="mix_out",
    )(o_lat, x, pbc, ga, wuv_pair, wb_a, w_out)


def _pair_block_diag(t):
    l, h, r, c = t.shape
    t = t.reshape(l, h // 2, 2, r, c)
    z = jnp.zeros((l, h // 2, r, c), t.dtype)
    top = jnp.concatenate([t[:, :, 0], z], axis=-1)
    bot = jnp.concatenate([z, t[:, :, 1]], axis=-1)
    return jnp.concatenate([top, bot], axis=-2)


def _rope_tables(pos, rope_dim):
    half = rope_dim // 2
    inv = ROPE_THETA ** (-jnp.arange(half, dtype=F32) / half)
    ang = pos.astype(F32)[:, None] * inv[None, :]
    cos, sin = jnp.cos(ang), jnp.sin(ang)
    reps = LANES // rope_dim
    return (jnp.tile(jnp.concatenate([cos, cos], axis=-1), (1, reps)),
            jnp.tile(jnp.concatenate([-sin, sin], axis=-1), (1, reps)))


def kernel(x_prompt, x_sample, cache_kv_latent, cache_k_rope, state_conv, page_table, ffn1_norm, ffn1_w_gate_up, ffn1_w_down, mix_norm, w_in, gate_bias, q_norm, w_uq, kv_norm, w_uk, w_uv, conv_w, sgu_ln_g, sgu_ln_b, sgu_w_spatial, sgu_b_spatial, w_branch, w_out, ffn2_norm, ffn2_w_gate_up, ffn2_w_down, final_norm):
    bp, s, d_model = x_prompt.shape
    bs, t_new, _ = x_sample.shape
    depth = w_in.shape[0]
    q_rank = q_norm.shape[-1]
    kv_rank = kv_norm.shape[-1]
    rope_dim = cache_k_rope.shape[-1]
    n_heads, nope_dim = w_uk.shape[2], w_uk.shape[3]
    v_dim = w_uv.shape[3]
    cw_dim = conv_w.shape[-1]
    conv_k = conv_w.shape[1]
    sgu_w = sgu_ln_g.shape[-1]
    n_groups = sgu_w_spatial.shape[1]
    group_dim = sgu_w // n_groups
    n_branch = w_branch.shape[1]
    past_len = page_table.shape[1] * cache_kv_latent.shape[2]
    assert conv_k == 3 and n_branch == 3 and s >= SGU_CHUNK and t_new < SGU_CHUNK
    assert w_uq.shape[3] == nope_dim + rope_dim and 2 * nope_dim == LANES and 2 * v_dim == LANES
    dims = (d_model, q_rank, kv_rank, rope_dim, cw_dim, sgu_w, n_heads, n_groups)
    cdt = MXU_DTYPE

    o_kr = q_rank + kv_rank
    w_in_p = jnp.concatenate(
        [w_in[..., :o_kr], w_in[..., o_kr + rope_dim:], w_in[..., o_kr:o_kr + rope_dim],
         jnp.zeros((depth, d_model, LANES - rope_dim), w_in.dtype)], axis=-1).astype(cdt)
    q_scale = (nope_dim + rope_dim) ** -0.5 * math.log2(math.e)
    wq_nope = w_uq[..., :nope_dim].reshape(depth, q_rank, n_heads * nope_dim).astype(cdt)
    wq_rope = w_uq[..., nope_dim:].reshape(depth, q_rank, n_heads * rope_dim).astype(cdt)
    wuk_pair = _pair_block_diag(w_uk.transpose(0, 2, 3, 1)).astype(cdt)
    wuv_pair = _pair_block_diag(w_uv.transpose(0, 2, 1, 3)).astype(cdt)
    wb = w_branch.astype(cdt)
    w_out_c = w_out.astype(cdt)
    f1gu, f1d = ffn1_w_gate_up.astype(cdt), ffn1_w_down.astype(cdt)
    f2gu, f2d = ffn2_w_gate_up.astype(cdt), ffn2_w_down.astype(cdt)

    bsp = jnp.repeat(sgu_b_spatial[:, :, :SGU_CHUNK].transpose(0, 2, 1), group_dim, axis=-1)
    w4 = jnp.tril(sgu_w_spatial[:, :, :t_new, :t_new])
    tabs = []
    for dlt in range(t_new):
        diag = jnp.diagonal(w4, offset=-dlt, axis1=2, axis2=3)
        diag = jnp.pad(diag, ((0, 0), (0, 0), (dlt, 0)))
        tabs.append(jnp.repeat(diag.transpose(0, 2, 1), group_dim, axis=-1))
    reps8 = SUBLANES // t_new
    tab_s = jnp.tile(jnp.stack(tabs, axis=1), (1, 1, reps8, 1))
    btab_s = jnp.tile(jnp.repeat(sgu_b_spatial[:, :, :t_new].transpose(0, 2, 1), group_dim, axis=-1),
                      (1, reps8, 1))

    cos_p, sin_p = _rope_tables(jnp.arange(s), rope_dim)
    cos_s, sin_s = _rope_tables(past_len + jnp.arange(t_new), rope_dim)
    cos_s, sin_s = jnp.tile(cos_s, (bs, 1)), jnp.tile(sin_s, (bs, 1))

    cache_rope_t = jnp.swapaxes(cache_k_rope, 2, 3)

    xp = x_prompt.reshape(bp * s, d_model)
    xs = x_sample.reshape(bs * t_new, d_model)
    qw = kv_rank + LANES
    lat_p, kr_p, conv_p, lat_s, kr_s, conv_s, v_s = [], [], [], [], [], [], []
    for l in range(depth):
        w = {"mix_norm": mix_norm[l].reshape(1, -1), "w_in": w_in_p[l], "q_norm": q_norm[l].reshape(1, -1),
             "wq_nope": wq_nope[l], "wq_rope": wq_rope[l], "wuk_pair": wuk_pair[l],
             "kv_norm": kv_norm[l].reshape(1, -1), "conv_w": conv_w[l],
             "ln_g": sgu_ln_g[l].reshape(1, -1), "ln_b": sgu_ln_b[l].reshape(1, -1),
             "wb_b": wb[l, 1], "wb_c": wb[l, 2], "gate_bias": gate_bias[l],
             "wsp": sgu_w_spatial[l][:, :SGU_CHUNK, :SGU_CHUNK], "bsp": bsp[l]}
        last = l == depth - 1

        xp = _ffn(xp, ffn1_norm[l], f1gu[l], f1d[l])
        q, kk, ckv, kr, pbc, ga, nconv = _mix_in(xp, cos_p, sin_p, w, dims, s, q_scale)
        o = _attn_prompt(q, kk.reshape(bp, s, qw), kv_rank)
        xp = _mix_out(o, xp, pbc, ga, wuv_pair[l], wb[l, 0], w_out_c[l])
        xp = _ffn(xp, ffn2_norm[l], f2gu[l], f2d[l], final_norm if last else None)
        lat_p.append(ckv.reshape(bp, s, kv_rank))
        kr_p.append(kr.reshape(bp, s, rope_dim))
        conv_p.append(nconv)

        st = state_conv[l]
        s0 = jnp.repeat(st[:, 0], t_new, axis=0)
        s1 = jnp.repeat(st[:, 1], t_new, axis=0)
        xs = _ffn(xs, ffn1_norm[l], f1gu[l], f1d[l])
        q, kk, ckv, kr, pbc, ga, u, v = _mix_in(xs, cos_s, sin_s, w, dims, t_new, q_scale,
                                                sample_extra=(s0, s1, tab_s[l], btab_s[l]))
        o = _attn_sample(page_table, q.reshape(bs, t_new * n_heads, qw), kk.reshape(bs, t_new, qw),
                         cache_kv_latent, cache_rope_t, l, n_heads, kv_rank)
        xs = _mix_out(o.reshape(bs * t_new, n_heads * kv_rank), xs, pbc, ga, wuv_pair[l], wb[l, 0], w_out_c[l])
        xs = _ffn(xs, ffn2_norm[l], f2gu[l], f2d[l], final_norm if last else None)
        lat_s.append(ckv.reshape(bs, t_new, kv_rank))
        kr_s.append(kr.reshape(bs, t_new, rope_dim))
        conv_s.append(u.reshape(bs, t_new, cw_dim)[:, t_new - (conv_k - 1):])
        v_s.append(v.reshape(bs, t_new, sgu_w))

    return (xp.reshape(bp, s, d_model), xs.reshape(bs, t_new, d_model),
            jnp.stack(lat_p), jnp.stack(kr_p), jnp.stack(conv_p),
            jnp.stack(lat_s), jnp.stack(kr_s), jnp.stack(conv_s), jnp.stack(v_s))
```

```python
import functools
import math

import jax
import jax.numpy as jnp
from jax import lax
from jax.experimental import pallas as pl
from jax.experimental.pallas import tpu as pltpu

F32 = jnp.float32
MXU_DTYPE = jnp.bfloat16

EPS = 1e-6
ROPE_THETA = 10000.0
SGU_CHUNK = 128
LANES = 128
SUBLANES = 8
VMEM_LIMIT_BYTES = 56 * 1024 * 1024

FFN_ROW_TILE = 512
FFN_COL_CHUNK = 256
MIX_ROW_TILE = 512
OUT_ROW_TILE = 512
ATTN_Q_TILE = 256


def _dot(a, b):
    return jnp.dot(a, b, preferred_element_type=F32)


def _dot_nt(a, b):
    return lax.dot_general(a, b, (((1,), (1,)), ((), ())), preferred_element_type=F32)


def _rms(x):
    return x * lax.rsqrt(jnp.mean(x * x, axis=-1, keepdims=True) + EPS)


def _const_spec(shape):
    zeros = (0,) * len(shape)
    return pl.BlockSpec(shape, lambda *_: zeros, pipeline_mode=pl.Buffered(1))


def _layer_spec(arr, *lead):
    tail = arr.shape[len(lead):]
    index = tuple(lead) + (0,) * len(tail)
    return pl.BlockSpec((None,) * len(lead) + tail, lambda *_: index, pipeline_mode=pl.Buffered(1))


def _params(n_grid_axes):
    return pltpu.CompilerParams(
        dimension_semantics=("arbitrary",) * n_grid_axes,
        vmem_limit_bytes=VMEM_LIMIT_BYTES)


def _ffn_half_step(x, g_ref, wgu_ref, wd_ref, act_ref, chunk):
    ffn_dim = wd_ref.shape[0]
    xn = (_rms(x) * g_ref[...]).astype(MXU_DTYPE)
    for c in range(ffn_dim // chunk):
        a = _dot(xn, wgu_ref[:, c * chunk:(c + 1) * chunk])
        b = _dot(xn, wgu_ref[:, ffn_dim + c * chunk:ffn_dim + (c + 1) * chunk])
        act_ref[:, c * chunk:(c + 1) * chunk] = (a * jax.nn.sigmoid(a) * b).astype(MXU_DTYPE)
    return x + 0.5 * _dot(act_ref[...], wd_ref[...])


def _ffn_body(x_ref, g_ref, wgu_ref, wd_ref, o_ref, act_ref, *, chunk):
    o_ref[...] = _ffn_half_step(x_ref[...], g_ref, wgu_ref, wd_ref, act_ref, chunk)


def _ffn(x, g, wgu_all, wd_all, layer):
    n, d = x.shape
    ffn_dim = wd_all.shape[1]
    tm = min(FFN_ROW_TILE, n)
    assert n % tm == 0 and ffn_dim % FFN_COL_CHUNK == 0
    return pl.pallas_call(
        functools.partial(_ffn_body, chunk=FFN_COL_CHUNK),
        grid=(n // tm,),
        in_specs=[pl.BlockSpec((tm, d), lambda i: (i, 0)), _const_spec((1, d)),
                  _layer_spec(wgu_all, layer), _layer_spec(wd_all, layer)],
        out_specs=pl.BlockSpec((tm, d), lambda i: (i, 0)),
        out_shape=jax.ShapeDtypeStruct((n, d), F32),
        scratch_shapes=[pltpu.VMEM((tm, ffn_dim), MXU_DTYPE)],
        compiler_params=_params(1),
        name="ffn",
    )(x, g.reshape(1, d), wgu_all, wd_all)


def _rope_rotate(x, cos, sin_signed):
    half = 16
    lane = lax.broadcasted_iota(jnp.int32, x.shape, 1)
    up = pltpu.roll(x, LANES - half, 1)
    down = pltpu.roll(x, half, 1)
    partner = jnp.where((lane % (2 * half)) < half, up, down)
    return x * cos + partner * sin_signed


def _mix_in_body(*refs, dims, sample, tiles_per_seq, q_scale):
    (d_model, q_rank, kv_rank, rope_dim, conv_w, sgu_w, n_heads, n_groups) = dims
    it = iter(refs)
    x_ref, cos_ref, sin_ref = next(it), next(it), next(it)
    if sample:
        s0_ref, s1_ref, tab_ref, btab_ref = next(it), next(it), next(it), next(it)
    else:
        wsp_ref, bsp_ref = next(it), next(it)
    (mixn_ref, win_ref, qn_ref, wqn_ref, wqr_ref, wuk_ref, kvn_ref, cw_ref, lng_ref, lnb_ref,
     wbb_ref, wbc_ref, gb_ref) = [next(it) for _ in range(13)]
    q_out, k_out, ckv_out, kr_out, pbc_out, ga_out = [next(it) for _ in range(6)]
    if sample:
        u_out, v_out = next(it), next(it)
        ubuf, vbuf = next(it), next(it)
    else:
        nc_out = next(it)
        ubuf = next(it)

    tm = x_ref.shape[0]
    halo = SUBLANES
    j = pl.program_id(0) % tiles_per_seq

    o_q = 0
    o_kv = o_q + q_rank
    o_cb = o_kv + kv_rank
    o_cc = o_cb + conv_w
    o_cx = o_cc + conv_w
    o_su = o_cx + conv_w
    o_sv = o_su + sgu_w
    o_g = o_sv + sgu_w
    o_kr = o_g + 3 * d_model

    def proj(lo, width):
        return _dot(xn, win_ref[:, lo:lo + width])

    xn = (_rms(x_ref[...]) * mixn_ref[...]).astype(MXU_DTYPE)
    cos = cos_ref[...]
    sin = sin_ref[...]

    cq = (_rms(proj(o_q, q_rank)) * qn_ref[...]).astype(MXU_DTYPE)
    q_nope = _dot(cq, wqn_ref[...]).astype(MXU_DTYPE)
    q_rope = _dot(cq, wqr_ref[...])
    qw = kv_rank + LANES
    heads_per_block = LANES // rope_dim
    lane = lax.broadcasted_iota(jnp.int32, (tm, LANES), 1)
    rot = [_rope_rotate(q_rope[:, b * LANES:(b + 1) * LANES], cos, sin) * q_scale
           for b in range(n_heads // heads_per_block)]
    for p in range(n_heads // 2):
        ql = _dot(q_nope[:, p * LANES:(p + 1) * LANES], wuk_ref[p]) * q_scale
        for e in range(2):
            h = 2 * p + e
            lat = ql[:, e * kv_rank:(e + 1) * kv_rank].astype(q_out.dtype)
            blk, pos = divmod(h, heads_per_block)
            r = rot[blk]
            if pos:
                r = pltpu.roll(r, LANES - pos * rope_dim, 1)
            r = jnp.where(lane < rope_dim, r, 0.0).astype(q_out.dtype)
            if sample:
                q_out[:, h * qw:h * qw + kv_rank] = lat
                q_out[:, h * qw + kv_rank:(h + 1) * qw] = r
            else:
                q_out[0, h, :, :kv_rank] = lat
                q_out[0, h, :, kv_rank:] = r

    ckv = _rms(proj(o_kv, kv_rank)) * kvn_ref[...]
    ckv_out[...] = ckv
    kr = _rope_rotate(proj(o_kr, LANES), cos, sin)
    kr_out[...] = kr[:, :rope_dim]
    k_out[:, :kv_rank] = ckv.astype(k_out.dtype)
    k_out[:, kv_rank:] = kr.astype(k_out.dtype)

    u = proj(o_cc, conv_w) * proj(o_cx, conv_w)
    if sample:
        ubuf[0:halo, :] = jnp.zeros((halo, conv_w), F32)
    else:
        @pl.when(j == 0)
        def _():
            ubuf[0:halo, :] = jnp.zeros((halo, conv_w), F32)
    ubuf[halo:halo + tm, :] = u
    u1 = ubuf[halo - 1:halo - 1 + tm, :]
    u2 = ubuf[halo - 2:halo - 2 + tm, :]
    if sample:
        t_len = tab_ref.shape[0]
        t = lax.broadcasted_iota(jnp.int32, (tm, conv_w), 0) % t_len
        u1 = jnp.where(t == 0, s1_ref[...], u1)
        u2 = jnp.where(t == 0, s0_ref[...], jnp.where(t == 1, s1_ref[...], u2))
        u_out[...] = u
    else:
        ubuf[0:halo, :] = u[tm - halo:, :]

        @pl.when(j == tiles_per_seq - 1)
        def _():
            nc_out[0] = u[tm - 2:, :]
    cw = cw_ref[...]
    out_b = proj(o_cb, conv_w) * (cw[0:1] * u2 + cw[1:2] * u1 + cw[2:3] * u)

    gu = jax.nn.gelu(proj(o_su, sgu_w))
    gv = jax.nn.gelu(proj(o_sv, sgu_w))
    mu = jnp.mean(gv, axis=-1, keepdims=True)
    gc = gv - mu
    v = gc * lax.rsqrt(jnp.mean(gc * gc, axis=-1, keepdims=True) + EPS) * lng_ref[...] + lnb_ref[...]
    if sample:
        v_out[...] = v
        t_len = tab_ref.shape[0]
        vbuf[0:halo, :] = jnp.zeros((halo, sgu_w), F32)
        vbuf[halo:halo + tm, :] = v
        reps = tm // SUBLANES
        mix = btab_ref[...][None] + tab_ref[0][None] * v.reshape(reps, SUBLANES, sgu_w)
        for dlt in range(1, t_len):
            vs = vbuf[halo - dlt:halo - dlt + tm, :].reshape(reps, SUBLANES, sgu_w)
            mix = mix + tab_ref[dlt][None] * vs
        mix = mix.reshape(tm, sgu_w)
    else:
        gd = sgu_w // n_groups
        gpb = LANES // gd
        row = lax.broadcasted_iota(jnp.int32, (SGU_CHUNK, SGU_CHUNK), 0)
        col = lax.broadcasted_iota(jnp.int32, (SGU_CHUNK, SGU_CHUNK), 1)
        wt = [jnp.where(row >= col, wsp_ref[g], 0.0).astype(MXU_DTYPE) for g in range(n_groups)]
        clane = lax.broadcasted_iota(jnp.int32, (SGU_CHUNK, LANES), 1)
        vb16 = v.astype(MXU_DTYPE)
        chunks = []
        for ch in range(tm // SGU_CHUNK):
            blocks = []
            for b in range(sgu_w // LANES):
                vb = vb16[ch * SGU_CHUNK:(ch + 1) * SGU_CHUNK, b * LANES:(b + 1) * LANES]
                res = _dot(wt[b * gpb], vb)
                for e in range(1, gpb):
                    res = jnp.where(clane >= e * gd, _dot(wt[b * gpb + e], vb), res)
                blocks.append(res)
            chunks.append(jnp.concatenate(blocks, axis=-1) + bsp_ref[...])
        mix = jnp.concatenate(chunks, axis=0)
    out_c = gu * mix

    gb = gb_ref[...]
    g_a = jax.nn.sigmoid(proj(o_g, d_model) + gb[0:1])
    g_b = jax.nn.sigmoid(proj(o_g + d_model, d_model) + gb[1:2])
    g_c = jax.nn.sigmoid(proj(o_g + 2 * d_model, d_model) + gb[2:3])
    ga_out[...] = g_a.astype(ga_out.dtype)
    pbc_out[...] = (g_b * _dot(out_b.astype(MXU_DTYPE), wbb_ref[...])
                    + g_c * _dot(out_c.astype(MXU_DTYPE), wbc_ref[...])).astype(pbc_out.dtype)


def _mix_in(x, cos, sin, w, dims, seq_len, q_scale, sample_extra=None):
    (d_model, q_rank, kv_rank, rope_dim, conv_w, sgu_w, n_heads, n_groups) = dims
    n = x.shape[0]
    sample = sample_extra is not None
    if sample:
        tm = min(MIX_ROW_TILE, n)
        assert tm % seq_len == 0 and SUBLANES % seq_len == 0 and seq_len >= 2
        tiles_per_seq = 1
        tab_index = lambda i: (i, 0)
    else:
        tm = min(MIX_ROW_TILE, seq_len)
        assert seq_len % tm == 0 and tm % SGU_CHUNK == 0
        tiles_per_seq = seq_len // tm
        tab_index = lambda i: (i % tiles_per_seq, 0)
    assert n % tm == 0 and LANES % rope_dim == 0 and n_heads % (LANES // rope_dim) == 0
    assert n_heads % 2 == 0 and rope_dim == 32 and LANES % (sgu_w // n_groups) == 0
    qw = kv_rank + LANES
    row = lambda width: pl.BlockSpec((tm, width), lambda i: (i, 0))

    in_specs = [row(d_model), pl.BlockSpec((tm, LANES), tab_index), pl.BlockSpec((tm, LANES), tab_index)]
    args = [x, cos, sin]
    if sample:
        s0, s1, tab, btab = sample_extra
        in_specs += [row(conv_w), row(conv_w), _const_spec(tab.shape), _const_spec(btab.shape)]
        args += [s0, s1, tab, btab]
    else:
        in_specs += [_const_spec(w["wsp"].shape), _const_spec(w["bsp"].shape)]
        args += [w["wsp"], w["bsp"]]
    names = ["mix_norm", "w_in", "q_norm", "wq_nope", "wq_rope", "wuk_pair", "kv_norm", "conv_w",
             "ln_g", "ln_b", "wb_b", "wb_c", "gate_bias"]
    for k in names:
        if isinstance(w[k], tuple):
            arr, lead = w[k]
            in_specs.append(_layer_spec(arr, *lead))
            args.append(arr)
        else:
            in_specs.append(_const_spec(w[k].shape))
            args.append(w[k])

    if sample:
        q_shape, q_spec = (n, n_heads * qw), row(n_heads * qw)
    else:
        q_shape = (n // seq_len, n_heads, seq_len, qw)
        q_spec = pl.BlockSpec((1, n_heads, tm, qw),
                              lambda i: (i // tiles_per_seq, 0, i % tiles_per_seq, 0))
    out_shape = [jax.ShapeDtypeStruct(q_shape, MXU_DTYPE),
                 jax.ShapeDtypeStruct((n, qw), MXU_DTYPE),
                 jax.ShapeDtypeStruct((n, kv_rank), F32),
                 jax.ShapeDtypeStruct((n, rope_dim), F32),
                 jax.ShapeDtypeStruct((n, d_model), MXU_DTYPE),
                 jax.ShapeDtypeStruct((n, d_model), MXU_DTYPE)]
    out_specs = [q_spec, row(qw), row(kv_rank), row(rope_dim), row(d_model), row(d_model)]
    scratch = [pltpu.VMEM((tm + SUBLANES, conv_w), F32)]
    if sample:
        out_shape += [jax.ShapeDtypeStruct((n, conv_w), F32), jax.ShapeDtypeStruct((n, sgu_w), F32)]
        out_specs += [row(conv_w), row(sgu_w)]
        scratch.append(pltpu.VMEM((tm + SUBLANES, sgu_w), F32))
    else:
        n_seq = n // seq_len
        out_shape.append(jax.ShapeDtypeStruct((n_seq, 2, conv_w), F32))
        out_specs.append(pl.BlockSpec((1, 2, conv_w), lambda i: (i // tiles_per_seq, 0, 0)))
    return pl.pallas_call(
        functools.partial(_mix_in_body, dims=dims, sample=sample, tiles_per_seq=tiles_per_seq,
                          q_scale=q_scale),
        grid=(n // tm,),
        in_specs=in_specs,
        out_specs=out_specs,
        out_shape=out_shape,
        scratch_shapes=scratch,
        compiler_params=_params(1),
        name="mix_in_sample" if sample else "mix_in_prompt",
    )(*args)


def _attn_prompt_body(q_ref, k_ref, o_ref, m_ref, l_ref, acc_ref, s0_ref, s1_ref, *, kv_rank):
    i = pl.program_id(1)
    heads, tq, qw = q_ref.shape[1:]
    rows = heads * tq
    kb = tq
    m_ref[...] = jnp.full(m_ref.shape, -jnp.inf, F32)
    l_ref[...] = jnp.zeros(l_ref.shape, F32)
    acc_ref[...] = jnp.zeros(acc_ref.shape, F32)

    def kblock(j):
        return k_ref[0, pl.ds(pl.multiple_of(j * kb, kb), kb), :]

    def scores(j, s_ref):
        s_ref[...] = _dot_nt(q_ref[0].reshape(rows, qw), kblock(j))

    def update(j, s_ref, diagonal):
        s = s_ref[...]
        if diagonal:
            t = lax.broadcasted_iota(jnp.int32, (heads, tq, kb), 1).reshape(rows, kb)
            c = lax.broadcasted_iota(jnp.int32, (rows, kb), 1)
            s = jnp.where(c <= t, s, -jnp.inf)
        m_prev = m_ref[...]
        m_new = jnp.maximum(m_prev, jnp.max(s, axis=-1, keepdims=True))
        alpha = jnp.exp2(m_prev - m_new)
        p = jnp.exp2(s - jnp.tile(m_new, (1, kb // LANES)))
        p_lanes = p[:, :LANES]
        for b in range(1, kb // LANES):
            p_lanes = p_lanes + p[:, b * LANES:(b + 1) * LANES]
        l_ref[...] = alpha * l_ref[...] + p_lanes
        m_ref[...] = m_new
        acc_ref[...] = (jnp.tile(alpha, (1, kv_rank // LANES)) * acc_ref[...]
                        + _dot(p.astype(k_ref.dtype), kblock(j)[:, :kv_rank]))

    scores(0, s0_ref)

    def pair(t, carry):
        j = 2 * t
        scores(j + 1, s1_ref)
        update(j, s0_ref, False)
        scores(j + 2, s0_ref)
        update(j + 1, s1_ref, False)
        return carry

    lax.fori_loop(0, i // 2, pair, 0)

    @pl.when(i % 2 == 1)
    def _():
        scores(i, s1_ref)
        update(i - 1, s0_ref, False)
        update(i, s1_ref, True)

    @pl.when(i % 2 == 0)
    def _():
        update(i, s0_ref, True)

    o = acc_ref[...] / jnp.sum(l_ref[...], axis=-1, keepdims=True)
    o_ref[0] = o.astype(o_ref.dtype).reshape(heads, tq, kv_rank)


def _attn_prompt(q, k, kv_rank):
    b, heads, s, qw = q.shape
    tq = min(ATTN_Q_TILE, s)
    assert s % tq == 0 and tq % LANES == 0 and kv_rank % LANES == 0
    rows = tq * heads
    return pl.pallas_call(
        functools.partial(_attn_prompt_body, kv_rank=kv_rank),
        grid=(b, s // tq),
        in_specs=[pl.BlockSpec((1, heads, tq, qw), lambda bi, i: (bi, 0, i, 0)),
                  pl.BlockSpec((1, s, qw), lambda bi, i: (bi, 0, 0))],
        out_specs=pl.BlockSpec((1, heads, tq, kv_rank), lambda bi, i: (bi, 0, i, 0)),
        out_shape=jax.ShapeDtypeStruct((b, heads, s, kv_rank), MXU_DTYPE),
        scratch_shapes=[pltpu.VMEM((rows, LANES), F32), pltpu.VMEM((rows, LANES), F32),
                        pltpu.VMEM((rows, kv_rank), F32),
                        pltpu.VMEM((rows, tq), F32), pltpu.VMEM((rows, tq), F32)],
        compiler_params=_params(2),
        name="attn_prompt",
    )(q, k)


def _attn_sample_body(pt_ref, q_ref, kn_ref, lat_hbm, rope_hbm, o_ref, cbuf, rbuf, sems,
                      *, layer, heads, kv_rank):
    b = pl.program_id(0)
    nb = pl.num_programs(0)
    n_pages = cbuf.shape[1]
    page = cbuf.shape[2]
    rope_dim = rbuf.shape[1]

    def copies(bi, slot, pg):
        src = pt_ref[bi, pg]
        return (pltpu.make_async_copy(lat_hbm.at[layer, src], cbuf.at[slot, pg], sems.at[0, slot]),
                pltpu.make_async_copy(rope_hbm.at[layer, src],
                                      rbuf.at[slot, :, pl.ds(pg * page, page)], sems.at[1, slot]))

    def start_all(bi, slot):
        for pg in range(n_pages):
            for cp in copies(bi, slot, pg):
                cp.start()

    def wait_all(bi, slot):
        for pg in range(n_pages):
            for cp in copies(bi, slot, pg):
                cp.wait()

    slot = b % 2

    @pl.when(b == 0)
    def _():
        start_all(0, 0)

    @pl.when(b + 1 < nb)
    def _():
        start_all(b + 1, 1 - slot)

    wait_all(b, slot)

    q = q_ref[0]
    rows = q.shape[0]
    c = cbuf[slot].reshape(n_pages * page, kv_rank).astype(q.dtype)
    kr_t = rbuf[slot].astype(q.dtype)
    s = _dot_nt(q[:, :kv_rank], c) + _dot(q[:, kv_rank:kv_rank + rope_dim], kr_t)
    kn = kn_ref[0]
    t_new = kn.shape[0]
    s_new = _dot_nt(q, kn)
    r = lax.broadcasted_iota(jnp.int32, (rows, t_new), 0)
    cc = lax.broadcasted_iota(jnp.int32, (rows, t_new), 1)
    s_new = jnp.where(cc * heads <= r, s_new, -jnp.inf)
    m = jnp.maximum(jnp.max(s, axis=-1, keepdims=True), jnp.max(s_new, axis=-1, keepdims=True))
    p = jnp.exp2(s - m)
    p_new = jnp.exp2(s_new - m)
    l = jnp.sum(p, axis=-1, keepdims=True) + jnp.sum(p_new, axis=-1, keepdims=True)
    o = _dot(p.astype(q.dtype), c) + _dot(p_new.astype(q.dtype), kn[:, :kv_rank])
    o_ref[0] = (o / l).astype(o_ref.dtype)


def _attn_sample(page_table, q, k_new, cache_lat, cache_rope_t, layer, heads, kv_rank):
    bs, rows, qw = q.shape
    t_new = k_new.shape[1]
    n_pages = page_table.shape[1]
    page = cache_lat.shape[2]
    rope_dim = cache_rope_t.shape[2]
    grid_spec = pltpu.PrefetchScalarGridSpec(
        num_scalar_prefetch=1,
        grid=(bs,),
        in_specs=[pl.BlockSpec((1, rows, qw), lambda b, pt: (b, 0, 0)),
                  pl.BlockSpec((1, t_new, qw), lambda b, pt: (b, 0, 0)),
                  pl.BlockSpec(memory_space=pl.ANY),
                  pl.BlockSpec(memory_space=pl.ANY)],
        out_specs=pl.BlockSpec((1, rows, kv_rank), lambda b, pt: (b, 0, 0)),
        scratch_shapes=[pltpu.VMEM((2, n_pages, page, kv_rank), F32),
                        pltpu.VMEM((2, rope_dim, n_pages * page), F32),
                        pltpu.SemaphoreType.DMA((2, 2))])
    return pl.pallas_call(
        functools.partial(_attn_sample_body, layer=layer, heads=heads, kv_rank=kv_rank),
        grid_spec=grid_spec,
        out_shape=jax.ShapeDtypeStruct((bs, rows, kv_rank), MXU_DTYPE),
        compiler_params=_params(1),
        name="attn_sample",
    )(page_table, q, k_new, cache_lat, cache_rope_t)


def _mix_out_ffn_body(*refs, head_major, chunk, final):
    (o_ref, x_ref, pbc_ref, ga_ref, wuv_ref, wba_ref, wout_ref, g_ref, wgu_ref, wd_ref) = refs[:10]
    if final:
        fg_ref, out_ref, act_ref = refs[10:]
    else:
        out_ref, act_ref = refs[10:]
    n_pairs = wuv_ref.shape[0]
    kv = wuv_ref.shape[1] // 2
    parts = []
    for p in range(n_pairs):
        if head_major:
            parts.append(_dot(o_ref[0, 2 * p], wuv_ref[p, :kv, :])
                         + _dot(o_ref[0, 2 * p + 1], wuv_ref[p, kv:, :]))
        else:
            parts.append(_dot(o_ref[:, 2 * p * kv:(2 * p + 2) * kv], wuv_ref[p]))
    out_a = jnp.concatenate(parts, axis=-1).astype(MXU_DTYPE)
    merged = pbc_ref[...].astype(F32) + ga_ref[...].astype(F32) * _dot(out_a, wba_ref[...])
    x = x_ref[...] + _dot(merged.astype(MXU_DTYPE), wout_ref[...])
    out = _ffn_half_step(x, g_ref, wgu_ref, wd_ref, act_ref, chunk)
    if final:
        out = _rms(out) * fg_ref[...]
    out_ref[...] = out


def _mix_out_ffn(o_lat, x, pbc, ga, wuv_all, wb_all, w_out_all, g, wgu_all, wd_all, layer, final_g=None):
    n, d = x.shape
    ffn_dim = wd_all.shape[1]
    head_major = o_lat.ndim == 4
    final = final_g is not None
    row = lambda width: pl.BlockSpec((tm, width), lambda i: (i, 0))
    if head_major:
        _, heads, s, kv = o_lat.shape
        tm = min(OUT_ROW_TILE, s)
        assert s % tm == 0
        tps = s // tm
        o_spec = pl.BlockSpec((1, heads, tm, kv), lambda i: (i // tps, 0, i % tps, 0))
    else:
        tm = min(OUT_ROW_TILE, n)
        o_spec = row(o_lat.shape[1])
    assert n % tm == 0 and ffn_dim % FFN_COL_CHUNK == 0
    in_specs = [o_spec, row(d), row(d), row(d),
                _layer_spec(wuv_all, layer), _layer_spec(wb_all, layer, 0), _layer_spec(w_out_all, layer),
                _const_spec((1, d)), _layer_spec(wgu_all, layer), _layer_spec(wd_all, layer)]
    args = [o_lat, x, pbc, ga, wuv_all, wb_all, w_out_all, g.reshape(1, d), wgu_all, wd_all]
    if final:
        in_specs.append(_const_spec((1, d)))
        args.append(final_g.reshape(1, d))
    return pl.pallas_call(
        functools.partial(_mix_out_ffn_body, head_major=head_major, chunk=FFN_COL_CHUNK, final=final),
        grid=(n // tm,),
        in_specs=in_specs,
        out_specs=row(d),
        out_shape=jax.ShapeDtypeStruct((n, d), F32),
        scratch_shapes=[pltpu.VMEM((tm, ffn_dim), MXU_DTYPE)],
        compiler_params=_params(1),
        name="mix_out_ffn_final" if final else "mix_out_ffn",
    )(*args)


def _pair_block_diag(t):
    l, h, r, c = t.shape
    t = t.reshape(l, h // 2, 2, r, c)
    z = jnp.zeros((l, h // 2, r, c), t.dtype)
    top = jnp.concatenate([t[:, :, 0], z], axis=-1)
    bot = jnp.concatenate([z, t[:, :, 1]], axis=-1)
    return jnp.concatenate([top, bot], axis=-2)


def _rope_tables(pos, rope_dim):
    half = rope_dim // 2
    inv = ROPE_THETA ** (-jnp.arange(half, dtype=F32) / half)
    ang = pos.astype(F32)[:, None] * inv[None, :]
    cos, sin = jnp.cos(ang), jnp.sin(ang)
    reps = LANES // rope_dim
    return (jnp.tile(jnp.concatenate([cos, cos], axis=-1), (1, reps)),
            jnp.tile(jnp.concatenate([-sin, sin], axis=-1), (1, reps)))


def kernel(x_prompt, x_sample, cache_kv_latent, cache_k_rope, state_conv, page_table, ffn1_norm, ffn1_w_gate_up, ffn1_w_down, mix_norm, w_in, gate_bias, q_norm, w_uq, kv_norm, w_uk, w_uv, conv_w, sgu_ln_g, sgu_ln_b, sgu_w_spatial, sgu_b_spatial, w_branch, w_out, ffn2_norm, ffn2_w_gate_up, ffn2_w_down, final_norm):
    bp, s, d_model = x_prompt.shape
    bs, t_new, _ = x_sample.shape
    depth = w_in.shape[0]
    q_rank = q_norm.shape[-1]
    kv_rank = kv_norm.shape[-1]
    rope_dim = cache_k_rope.shape[-1]
    n_heads, nope_dim = w_uk.shape[2], w_uk.shape[3]
    v_dim = w_uv.shape[3]
    cw_dim = conv_w.shape[-1]
    conv_k = conv_w.shape[1]
    sgu_w = sgu_ln_g.shape[-1]
    n_groups = sgu_w_spatial.shape[1]
    group_dim = sgu_w // n_groups
    n_branch = w_branch.shape[1]
    past_len = page_table.shape[1] * cache_kv_latent.shape[2]
    assert conv_k == 3 and n_branch == 3 and s >= SGU_CHUNK and t_new < SGU_CHUNK
    assert w_uq.shape[3] == nope_dim + rope_dim and 2 * nope_dim == LANES and 2 * v_dim == LANES
    dims = (d_model, q_rank, kv_rank, rope_dim, cw_dim, sgu_w, n_heads, n_groups)
    cdt = MXU_DTYPE

    o_kr = q_rank + kv_rank
    w_in_p = jnp.concatenate(
        [w_in[..., :o_kr], w_in[..., o_kr + rope_dim:], w_in[..., o_kr:o_kr + rope_dim],
         jnp.zeros((depth, d_model, LANES - rope_dim), w_in.dtype)], axis=-1).astype(cdt)
    q_scale = (nope_dim + rope_dim) ** -0.5 * math.log2(math.e)
    wq_nope = w_uq[..., :nope_dim].reshape(depth, q_rank, n_heads * nope_dim).astype(cdt)
    wq_rope = w_uq[..., nope_dim:].reshape(depth, q_rank, n_heads * rope_dim).astype(cdt)
    wuk_pair = _pair_block_diag(w_uk.transpose(0, 2, 3, 1)).astype(cdt)
    wuv_pair = _pair_block_diag(w_uv.transpose(0, 2, 1, 3)).astype(cdt)
    wb = w_branch.astype(cdt)
    w_out_c = w_out.astype(cdt)
    f1gu, f1d = ffn1_w_gate_up.astype(cdt), ffn1_w_down.astype(cdt)
    f2gu, f2d = ffn2_w_gate_up.astype(cdt), ffn2_w_down.astype(cdt)

    bsp = jnp.repeat(sgu_b_spatial[:, :, :SGU_CHUNK].transpose(0, 2, 1), group_dim, axis=-1)
    w4 = jnp.tril(sgu_w_spatial[:, :, :t_new, :t_new])
    tabs = []
    for dlt in range(t_new):
        diag = jnp.diagonal(w4, offset=-dlt, axis1=2, axis2=3)
        diag = jnp.pad(diag, ((0, 0), (0, 0), (dlt, 0)))
        tabs.append(jnp.repeat(diag.transpose(0, 2, 1), group_dim, axis=-1))
    reps8 = SUBLANES // t_new
    tab_s = jnp.tile(jnp.stack(tabs, axis=1), (1, 1, reps8, 1))
    btab_s = jnp.tile(jnp.repeat(sgu_b_spatial[:, :, :t_new].transpose(0, 2, 1), group_dim, axis=-1),
                      (1, reps8, 1))

    cos_p, sin_p = _rope_tables(jnp.arange(s), rope_dim)
    cos_s, sin_s = _rope_tables(past_len + jnp.arange(t_new), rope_dim)
    cos_s, sin_s = jnp.tile(cos_s, (bs, 1)), jnp.tile(sin_s, (bs, 1))

    cache_rope_t = jnp.swapaxes(cache_k_rope, 2, 3)

    xp = x_prompt.reshape(bp * s, d_model)
    xs = x_sample.reshape(bs * t_new, d_model)
    qw = kv_rank + LANES
    lat_p, kr_p, conv_p, lat_s, kr_s, conv_s, v_s = [], [], [], [], [], [], []
    for l in range(depth):
        w = {"mix_norm": mix_norm[l].reshape(1, -1), "w_in": (w_in_p, (l,)), "q_norm": q_norm[l].reshape(1, -1),
             "wq_nope": (wq_nope, (l,)), "wq_rope": (wq_rope, (l,)), "wuk_pair": (wuk_pair, (l,)),
             "kv_norm": kv_norm[l].reshape(1, -1), "conv_w": conv_w[l],
             "ln_g": sgu_ln_g[l].reshape(1, -1), "ln_b": sgu_ln_b[l].reshape(1, -1),
             "wb_b": (wb, (l, 1)), "wb_c": (wb, (l, 2)), "gate_bias": gate_bias[l],
             "wsp": sgu_w_spatial[l][:, :SGU_CHUNK, :SGU_CHUNK], "bsp": bsp[l]}
        fin = final_norm if l == depth - 1 else None

        xp = _ffn(xp, ffn1_norm[l], f1gu, f1d, l)
        q, kk, ckv, kr, pbc, ga, nconv = _mix_in(xp, cos_p, sin_p, w, dims, s, q_scale)
        o = _attn_prompt(q, kk.reshape(bp, s, qw), kv_rank)
        xp = _mix_out_ffn(o, xp, pbc, ga, wuv_pair, wb, w_out_c, ffn2_norm[l], f2gu, f2d, l, fin)
        lat_p.append(ckv.reshape(bp, s, kv_rank))
        kr_p.append(kr.reshape(bp, s, rope_dim))
        conv_p.append(nconv)

        st = state_conv[l]
        s0 = jnp.repeat(st[:, 0], t_new, axis=0)
        s1 = jnp.repeat(st[:, 1], t_new, axis=0)
        xs = _ffn(xs, ffn1_norm[l], f1gu, f1d, l)
        q, kk, ckv, kr, pbc, ga, u, v = _mix_in(xs, cos_s, sin_s, w, dims, t_new, q_scale,
                                                sample_extra=(s0, s1, tab_s[l], btab_s[l]))
        o = _attn_sample(page_table, q.reshape(bs, t_new * n_heads, qw), kk.reshape(bs, t_new, qw),
                         cache_kv_latent, cache_rope_t, l, n_heads, kv_rank)
        xs = _mix_out_ffn(o.reshape(bs * t_new, n_heads * kv_rank), xs, pbc, ga, wuv_pair, wb, w_out_c,
                          ffn2_norm[l], f2gu, f2d, l, fin)
        lat_s.append(ckv.reshape(bs, t_new, kv_rank))
        kr_s.append(kr.reshape(bs, t_new, rope_dim))
        conv_s.append(u.reshape(bs, t_new, cw_dim)[:, t_new - (conv_k - 1):])
        v_s.append(v.reshape(bs, t_new, sgu_w))

    return (xp.reshape(bp, s, d_model), xs.reshape(bs, t_new, d_model),
            jnp.stack(lat_p), jnp.stack(kr_p), jnp.stack(conv_p),
            jnp.stack(lat_s), jnp.stack(kr_s), jnp.stack(conv_s), jnp.stack(v_s))
```

```python
import functools
import math

import jax
import jax.numpy as jnp
from jax import lax
from jax.experimental import pallas as pl
from jax.experimental.pallas import tpu as pltpu

F32 = jnp.float32
MXU_DTYPE = jnp.bfloat16

EPS = 1e-6
ROPE_THETA = 10000.0
SGU_CHUNK = 128
LANES = 128
SUBLANES = 8
VMEM_LIMIT_BYTES = 56 * 1024 * 1024

FFN_ROW_TILE = 512
FFN_COL_CHUNK = 256
MIX_ROW_TILE = 512
OUT_ROW_TILE = 512
ATTN_Q_TILE = 256


def _dot(a, b):
    return jnp.dot(a, b, preferred_element_type=F32)


def _dot_nt(a, b):
    return lax.dot_general(a, b, (((1,), (1,)), ((), ())), preferred_element_type=F32)


def _rms(x):
    return x * lax.rsqrt(jnp.mean(x * x, axis=-1, keepdims=True) + EPS)


def _const_spec(shape):
    zeros = (0,) * len(shape)
    return pl.BlockSpec(shape, lambda *_: zeros, pipeline_mode=pl.Buffered(1))


def _layer_spec(arr, *lead):
    tail = arr.shape[len(lead):]
    index = tuple(lead) + (0,) * len(tail)
    return pl.BlockSpec((None,) * len(lead) + tail, lambda *_: index, pipeline_mode=pl.Buffered(1))


def _params(n_grid_axes):
    return pltpu.CompilerParams(
        dimension_semantics=("arbitrary",) * n_grid_axes,
        vmem_limit_bytes=VMEM_LIMIT_BYTES)


def _ffn_half_step(x, g_ref, wgu_ref, wd_ref, act_ref, chunk):
    ffn_dim = wd_ref.shape[0]
    xn = (_rms(x) * g_ref[...]).astype(MXU_DTYPE)
    for c in range(ffn_dim // chunk):
        a = _dot(xn, wgu_ref[:, c * chunk:(c + 1) * chunk])
        b = _dot(xn, wgu_ref[:, ffn_dim + c * chunk:ffn_dim + (c + 1) * chunk])
        act_ref[:, c * chunk:(c + 1) * chunk] = (a * jax.nn.sigmoid(a) * b).astype(MXU_DTYPE)
    return x + 0.5 * _dot(act_ref[...], wd_ref[...])


def _ffn_body(x_ref, g_ref, wgu_ref, wd_ref, o_ref, act_ref, *, chunk):
    o_ref[...] = _ffn_half_step(x_ref[...], g_ref, wgu_ref, wd_ref, act_ref, chunk)


def _ffn(x, g, wgu_all, wd_all, layer):
    n, d = x.shape
    ffn_dim = wd_all.shape[1]
    tm = min(FFN_ROW_TILE, n)
    assert n % tm == 0 and ffn_dim % FFN_COL_CHUNK == 0
    return pl.pallas_call(
        functools.partial(_ffn_body, chunk=FFN_COL_CHUNK),
        grid=(n // tm,),
        in_specs=[pl.BlockSpec((tm, d), lambda i: (i, 0)), _const_spec((1, d)),
                  _layer_spec(wgu_all, layer), _layer_spec(wd_all, layer)],
        out_specs=pl.BlockSpec((tm, d), lambda i: (i, 0)),
        out_shape=jax.ShapeDtypeStruct((n, d), F32),
        scratch_shapes=[pltpu.VMEM((tm, ffn_dim), MXU_DTYPE)],
        compiler_params=_params(1),
        name="ffn",
    )(x, g.reshape(1, d), wgu_all, wd_all)


def _rope_rotate(x, cos, sin_signed):
    half = 16
    lane = lax.broadcasted_iota(jnp.int32, x.shape, 1)
    up = pltpu.roll(x, LANES - half, 1)
    down = pltpu.roll(x, half, 1)
    partner = jnp.where((lane % (2 * half)) < half, up, down)
    return x * cos + partner * sin_signed


def _mix_in_body(*refs, dims, sample, tiles_per_seq, q_scale):
    (d_model, q_rank, kv_rank, rope_dim, conv_w, sgu_w, n_heads, n_groups) = dims
    it = iter(refs)
    x_ref, cos_ref, sin_ref = next(it), next(it), next(it)
    if sample:
        s0_ref, s1_ref, tab_ref, btab_ref = next(it), next(it), next(it), next(it)
    else:
        wsp_ref, bsp_ref = next(it), next(it)
    (mixn_ref, win_ref, qn_ref, wqn_ref, wqr_ref, wuk_ref, kvn_ref, cw_ref, lng_ref, lnb_ref,
     wbb_ref, wbc_ref, gb_ref) = [next(it) for _ in range(13)]
    q_out, k_out, ckv_out, kr_out, pbc_out, ga_out = [next(it) for _ in range(6)]
    if sample:
        u_out, v_out = next(it), next(it)
        ubuf, vbuf = next(it), next(it)
    else:
        nc_out = next(it)
        ubuf = next(it)

    tm = x_ref.shape[0]
    halo = SUBLANES
    j = pl.program_id(0) % tiles_per_seq

    o_q = 0
    o_kv = o_q + q_rank
    o_cb = o_kv + kv_rank
    o_cc = o_cb + conv_w
    o_cx = o_cc + conv_w
    o_su = o_cx + conv_w
    o_sv = o_su + sgu_w
    o_g = o_sv + sgu_w
    o_kr = o_g + 3 * d_model

    def proj(lo, width):
        return _dot(xn, win_ref[:, lo:lo + width])

    xn = (_rms(x_ref[...]) * mixn_ref[...]).astype(MXU_DTYPE)
    cos = cos_ref[...]
    sin = sin_ref[...]

    cq = (_rms(proj(o_q, q_rank)) * qn_ref[...]).astype(MXU_DTYPE)
    q_nope = _dot(cq, wqn_ref[...]).astype(MXU_DTYPE)
    q_rope = _dot(cq, wqr_ref[...])
    qw = kv_rank + LANES
    heads_per_block = LANES // rope_dim
    lane = lax.broadcasted_iota(jnp.int32, (tm, LANES), 1)
    rot = [_rope_rotate(q_rope[:, b * LANES:(b + 1) * LANES], cos, sin) * q_scale
           for b in range(n_heads // heads_per_block)]
    for p in range(n_heads // 2):
        ql = _dot(q_nope[:, p * LANES:(p + 1) * LANES], wuk_ref[p]) * q_scale
        for e in range(2):
            h = 2 * p + e
            lat = ql[:, e * kv_rank:(e + 1) * kv_rank].astype(q_out.dtype)
            blk, pos = divmod(h, heads_per_block)
            r = rot[blk]
            if pos:
                r = pltpu.roll(r, LANES - pos * rope_dim, 1)
            r = jnp.where(lane < rope_dim, r, 0.0).astype(q_out.dtype)
            if sample:
                q_out[:, h * qw:h * qw + kv_rank] = lat
                q_out[:, h * qw + kv_rank:(h + 1) * qw] = r
            else:
                q_out[0, h, :, :kv_rank] = lat
                q_out[0, h, :, kv_rank:] = r

    ckv = _rms(proj(o_kv, kv_rank)) * kvn_ref[...]
    ckv_out[...] = ckv
    kr = _rope_rotate(proj(o_kr, LANES), cos, sin)
    kr_out[...] = kr[:, :rope_dim]
    k_out[:, :kv_rank] = ckv.astype(k_out.dtype)
    k_out[:, kv_rank:] = kr.astype(k_out.dtype)

    u = proj(o_cc, conv_w) * proj(o_cx, conv_w)
    if sample:
        ubuf[0:halo, :] = jnp.zeros((halo, conv_w), F32)
    else:
        @pl.when(j == 0)
        def _():
            ubuf[0:halo, :] = jnp.zeros((halo, conv_w), F32)
    ubuf[halo:halo + tm, :] = u
    u1 = ubuf[halo - 1:halo - 1 + tm, :]
    u2 = ubuf[halo - 2:halo - 2 + tm, :]
    if sample:
        t_len = tab_ref.shape[0]
        t = lax.broadcasted_iota(jnp.int32, (tm, conv_w), 0) % t_len
        u1 = jnp.where(t == 0, s1_ref[...], u1)
        u2 = jnp.where(t == 0, s0_ref[...], jnp.where(t == 1, s1_ref[...], u2))
        u_out[...] = u
    else:
        ubuf[0:halo, :] = u[tm - halo:, :]

        @pl.when(j == tiles_per_seq - 1)
        def _():
            nc_out[0] = u[tm - 2:, :]
    cw = cw_ref[...]
    out_b = proj(o_cb, conv_w) * (cw[0:1] * u2 + cw[1:2] * u1 + cw[2:3] * u)

    gu = jax.nn.gelu(proj(o_su, sgu_w))
    gv = jax.nn.gelu(proj(o_sv, sgu_w))
    mu = jnp.mean(gv, axis=-1, keepdims=True)
    gc = gv - mu
    v = gc * lax.rsqrt(jnp.mean(gc * gc, axis=-1, keepdims=True) + EPS) * lng_ref[...] + lnb_ref[...]
    if sample:
        v_out[...] = v
        t_len = tab_ref.shape[0]
        vbuf[0:halo, :] = jnp.zeros((halo, sgu_w), F32)
        vbuf[halo:halo + tm, :] = v
        reps = tm // SUBLANES
        mix = btab_ref[...][None] + tab_ref[0][None] * v.reshape(reps, SUBLANES, sgu_w)
        for dlt in range(1, t_len):
            vs = vbuf[halo - dlt:halo - dlt + tm, :].reshape(reps, SUBLANES, sgu_w)
            mix = mix + tab_ref[dlt][None] * vs
        mix = mix.reshape(tm, sgu_w)
    else:
        gd = sgu_w // n_groups
        gpb = LANES // gd
        row = lax.broadcasted_iota(jnp.int32, (SGU_CHUNK, SGU_CHUNK), 0)
        col = lax.broadcasted_iota(jnp.int32, (SGU_CHUNK, SGU_CHUNK), 1)
        wt = [jnp.where(row >= col, wsp_ref[g], 0.0).astype(MXU_DTYPE) for g in range(n_groups)]
        clane = lax.broadcasted_iota(jnp.int32, (SGU_CHUNK, LANES), 1)
        vb16 = v.astype(MXU_DTYPE)
        chunks = []
        for ch in range(tm // SGU_CHUNK):
            blocks = []
            for b in range(sgu_w // LANES):
                vb = vb16[ch * SGU_CHUNK:(ch + 1) * SGU_CHUNK, b * LANES:(b + 1) * LANES]
                res = _dot(wt[b * gpb], vb)
                for e in range(1, gpb):
                    res = jnp.where(clane >= e * gd, _dot(wt[b * gpb + e], vb), res)
                blocks.append(res)
            chunks.append(jnp.concatenate(blocks, axis=-1) + bsp_ref[...])
        mix = jnp.concatenate(chunks, axis=0)
    out_c = gu * mix

    gb = gb_ref[...]
    g_a = jax.nn.sigmoid(proj(o_g, d_model) + gb[0:1])
    g_b = jax.nn.sigmoid(proj(o_g + d_model, d_model) + gb[1:2])
    g_c = jax.nn.sigmoid(proj(o_g + 2 * d_model, d_model) + gb[2:3])
    ga_out[...] = g_a.astype(ga_out.dtype)
    pbc_out[...] = (g_b * _dot(out_b.astype(MXU_DTYPE), wbb_ref[...])
                    + g_c * _dot(out_c.astype(MXU_DTYPE), wbc_ref[...])).astype(pbc_out.dtype)


def _mix_in(x, cos, sin, w, dims, seq_len, q_scale, sample_extra=None):
    (d_model, q_rank, kv_rank, rope_dim, conv_w, sgu_w, n_heads, n_groups) = dims
    n = x.shape[0]
    sample = sample_extra is not None
    if sample:
        tm = min(MIX_ROW_TILE, n)
        assert tm % seq_len == 0 and SUBLANES % seq_len == 0 and seq_len >= 2
        tiles_per_seq = 1
        tab_index = lambda i: (i, 0)
    else:
        tm = min(MIX_ROW_TILE, seq_len)
        assert seq_len % tm == 0 and tm % SGU_CHUNK == 0
        tiles_per_seq = seq_len // tm
        tab_index = lambda i: (i % tiles_per_seq, 0)
    assert n % tm == 0 and LANES % rope_dim == 0 and n_heads % (LANES // rope_dim) == 0
    assert n_heads % 2 == 0 and rope_dim == 32 and LANES % (sgu_w // n_groups) == 0
    qw = kv_rank + LANES
    row = lambda width: pl.BlockSpec((tm, width), lambda i: (i, 0))

    in_specs = [row(d_model), pl.BlockSpec((tm, LANES), tab_index), pl.BlockSpec((tm, LANES), tab_index)]
    args = [x, cos, sin]
    if sample:
        s0, s1, tab, btab = sample_extra
        in_specs += [row(conv_w), row(conv_w), _const_spec(tab.shape), _const_spec(btab.shape)]
        args += [s0, s1, tab, btab]
    else:
        in_specs += [_const_spec(w["wsp"].shape), _const_spec(w["bsp"].shape)]
        args += [w["wsp"], w["bsp"]]
    names = ["mix_norm", "w_in", "q_norm", "wq_nope", "wq_rope", "wuk_pair", "kv_norm", "conv_w",
             "ln_g", "ln_b", "wb_b", "wb_c", "gate_bias"]
    for k in names:
        if isinstance(w[k], tuple):
            arr, lead = w[k]
            in_specs.append(_layer_spec(arr, *lead))
            args.append(arr)
        else:
            in_specs.append(_const_spec(w[k].shape))
            args.append(w[k])

    if sample:
        q_shape, q_spec = (n, n_heads * qw), row(n_heads * qw)
    else:
        q_shape = (n // seq_len, n_heads, seq_len, qw)
        q_spec = pl.BlockSpec((1, n_heads, tm, qw),
                              lambda i: (i // tiles_per_seq, 0, i % tiles_per_seq, 0))
    out_shape = [jax.ShapeDtypeStruct(q_shape, MXU_DTYPE),
                 jax.ShapeDtypeStruct((n, qw), MXU_DTYPE),
                 jax.ShapeDtypeStruct((n, kv_rank), F32),
                 jax.ShapeDtypeStruct((n, rope_dim), F32),
                 jax.ShapeDtypeStruct((n, d_model), MXU_DTYPE),
                 jax.ShapeDtypeStruct((n, d_model), MXU_DTYPE)]
    out_specs = [q_spec, row(qw), row(kv_rank), row(rope_dim), row(d_model), row(d_model)]
    scratch = [pltpu.VMEM((tm + SUBLANES, conv_w), F32)]
    if sample:
        out_shape += [jax.ShapeDtypeStruct((n, conv_w), F32), jax.ShapeDtypeStruct((n, sgu_w), F32)]
        out_specs += [row(conv_w), row(sgu_w)]
        scratch.append(pltpu.VMEM((tm + SUBLANES, sgu_w), F32))
    else:
        n_seq = n // seq_len
        out_shape.append(jax.ShapeDtypeStruct((n_seq, 2, conv_w), F32))
        out_specs.append(pl.BlockSpec((1, 2, conv_w), lambda i: (i // tiles_per_seq, 0, 0)))
    return pl.pallas_call(
        functools.partial(_mix_in_body, dims=dims, sample=sample, tiles_per_seq=tiles_per_seq,
                          q_scale=q_scale),
        grid=(n // tm,),
        in_specs=in_specs,
        out_specs=out_specs,
        out_shape=out_shape,
        scratch_shapes=scratch,
        compiler_params=_params(1),
        name="mix_in_sample" if sample else "mix_in_prompt",
    )(*args)


def _attn_prompt_body(q_ref, k_ref, o_ref, m_ref, l_ref, acc_ref, s0_ref, s1_ref, *, kv_rank):
    i = pl.program_id(1)
    heads, tq, qw = q_ref.shape[1:]
    rows = heads * tq
    kb = tq
    m_ref[...] = jnp.full(m_ref.shape, -jnp.inf, F32)
    l_ref[...] = jnp.zeros(l_ref.shape, F32)
    acc_ref[...] = jnp.zeros(acc_ref.shape, F32)

    def kblock(j):
        return k_ref[0, pl.ds(pl.multiple_of(j * kb, kb), kb), :]

    def scores(j, s_ref):
        s_ref[...] = _dot_nt(q_ref[0].reshape(rows, qw), kblock(j))

    def update(j, s_ref, diagonal):
        s = s_ref[...]
        if diagonal:
            t = lax.broadcasted_iota(jnp.int32, (heads, tq, kb), 1).reshape(rows, kb)
            c = lax.broadcasted_iota(jnp.int32, (rows, kb), 1)
            s = jnp.where(c <= t, s, -jnp.inf)
        m_prev = m_ref[...]
        m_new = jnp.maximum(m_prev, jnp.max(s, axis=-1, keepdims=True))
        alpha = jnp.exp2(m_prev - m_new)
        p = jnp.exp2(s - jnp.tile(m_new, (1, kb // LANES)))
        p_lanes = p[:, :LANES]
        for b in range(1, kb // LANES):
            p_lanes = p_lanes + p[:, b * LANES:(b + 1) * LANES]
        l_ref[...] = alpha * l_ref[...] + p_lanes
        m_ref[...] = m_new
        acc_ref[...] = (jnp.tile(alpha, (1, kv_rank // LANES)) * acc_ref[...]
                        + _dot(p.astype(k_ref.dtype), kblock(j)[:, :kv_rank]))

    scores(0, s0_ref)

    def pair(t, carry):
        j = 2 * t
        scores(j + 1, s1_ref)
        update(j, s0_ref, False)
        scores(j + 2, s0_ref)
        update(j + 1, s1_ref, False)
        return carry

    lax.fori_loop(0, i // 2, pair, 0)

    @pl.when(i % 2 == 1)
    def _():
        scores(i, s1_ref)
        update(i - 1, s0_ref, False)
        update(i, s1_ref, True)

    @pl.when(i % 2 == 0)
    def _():
        update(i, s0_ref, True)

    o = acc_ref[...] / jnp.sum(l_ref[...], axis=-1, keepdims=True)
    o_ref[0] = o.astype(o_ref.dtype).reshape(heads, tq, kv_rank)


def _attn_prompt(q, k, kv_rank):
    b, heads, s, qw = q.shape
    tq = min(ATTN_Q_TILE, s)
    assert s % tq == 0 and tq % LANES == 0 and kv_rank % LANES == 0
    rows = tq * heads
    return pl.pallas_call(
        functools.partial(_attn_prompt_body, kv_rank=kv_rank),
        grid=(b, s // tq),
        in_specs=[pl.BlockSpec((1, heads, tq, qw), lambda bi, i: (bi, 0, i, 0)),
                  pl.BlockSpec((1, s, qw), lambda bi, i: (bi, 0, 0))],
        out_specs=pl.BlockSpec((1, heads, tq, kv_rank), lambda bi, i: (bi, 0, i, 0)),
        out_shape=jax.ShapeDtypeStruct((b, heads, s, kv_rank), MXU_DTYPE),
        scratch_shapes=[pltpu.VMEM((rows, LANES), F32), pltpu.VMEM((rows, LANES), F32),
                        pltpu.VMEM((rows, kv_rank), F32),
                        pltpu.VMEM((rows, tq), F32), pltpu.VMEM((rows, tq), F32)],
        compiler_params=_params(2),
        name="attn_prompt",
    )(q, k)


def _attn_sample_body(pt_ref, q_ref, kn_ref, lat_hbm, rope_hbm, o_ref, cbuf, rbuf, sems,
                      *, layer, heads, kv_rank):
    b = pl.program_id(0)
    nb = pl.num_programs(0)
    n_pages = cbuf.shape[1]
    page = cbuf.shape[2]
    rope_dim = rbuf.shape[1]

    def copies(bi, slot, pg):
        src = pt_ref[bi, pg]
        return (pltpu.make_async_copy(lat_hbm.at[layer, src], cbuf.at[slot, pg], sems.at[0, slot]),
                pltpu.make_async_copy(rope_hbm.at[layer, src],
                                      rbuf.at[slot, :, pl.ds(pg * page, page)], sems.at[1, slot]))

    def start_all(bi, slot):
        for pg in range(n_pages):
            lat_cp, rope_cp = copies(bi, slot, pg)
            lat_cp.start(priority=1)
            rope_cp.start(priority=0)

    def wait_all(bi, slot):
        for pg in range(n_pages):
            for cp in copies(bi, slot, pg):
                cp.wait()

    slot = b % 2

    @pl.when(b == 0)
    def _():
        start_all(0, 0)

    @pl.when(b + 1 < nb)
    def _():
        start_all(b + 1, 1 - slot)

    wait_all(b, slot)

    q = q_ref[0]
    rows = q.shape[0]
    c = cbuf[slot].reshape(n_pages * page, kv_rank).astype(q.dtype)
    kr_t = rbuf[slot].astype(q.dtype)
    s = _dot_nt(q[:, :kv_rank], c) + _dot(q[:, kv_rank:kv_rank + rope_dim], kr_t)
    kn = kn_ref[0]
    t_new = kn.shape[0]
    s_new = _dot_nt(q, kn)
    r = lax.broadcasted_iota(jnp.int32, (rows, t_new), 0)
    cc = lax.broadcasted_iota(jnp.int32, (rows, t_new), 1)
    s_new = jnp.where(cc * heads <= r, s_new, -jnp.inf)
    m = jnp.maximum(jnp.max(s, axis=-1, keepdims=True), jnp.max(s_new, axis=-1, keepdims=True))
    p = jnp.exp2(s - m)
    p_new = jnp.exp2(s_new - m)
    l = jnp.sum(p, axis=-1, keepdims=True) + jnp.sum(p_new, axis=-1, keepdims=True)
    o = _dot(p.astype(q.dtype), c) + _dot(p_new.astype(q.dtype), kn[:, :kv_rank])
    o_ref[0] = (o / l).astype(o_ref.dtype)


def _attn_sample(page_table, q, k_new, cache_lat, cache_rope_t, layer, heads, kv_rank):
    bs, rows, qw = q.shape
    t_new = k_new.shape[1]
    n_pages = page_table.shape[1]
    page = cache_lat.shape[2]
    rope_dim = cache_rope_t.shape[2]
    grid_spec = pltpu.PrefetchScalarGridSpec(
        num_scalar_prefetch=1,
        grid=(bs,),
        in_specs=[pl.BlockSpec((1, rows, qw), lambda b, pt: (b, 0, 0)),
                  pl.BlockSpec((1, t_new, qw), lambda b, pt: (b, 0, 0)),
                  pl.BlockSpec(memory_space=pl.ANY),
                  pl.BlockSpec(memory_space=pl.ANY)],
        out_specs=pl.BlockSpec((1, rows, kv_rank), lambda b, pt: (b, 0, 0)),
        scratch_shapes=[pltpu.VMEM((2, n_pages, page, kv_rank), F32),
                        pltpu.VMEM((2, rope_dim, n_pages * page), F32),
                        pltpu.SemaphoreType.DMA((2, 2))])
    return pl.pallas_call(
        functools.partial(_attn_sample_body, layer=layer, heads=heads, kv_rank=kv_rank),
        grid_spec=grid_spec,
        out_shape=jax.ShapeDtypeStruct((bs, rows, kv_rank), MXU_DTYPE),
        compiler_params=_params(1),
        name="attn_sample",
    )(page_table, q, k_new, cache_lat, cache_rope_t)


def _mix_out_ffn_body(*refs, head_major, chunk, final):
    (o_ref, x_ref, pbc_ref, ga_ref, wuv_ref, wba_ref, wout_ref, g_ref, wgu_ref, wd_ref) = refs[:10]
    if final:
        fg_ref, out_ref, act_ref = refs[10:]
    else:
        out_ref, act_ref = refs[10:]
    n_pairs = wuv_ref.shape[0]
    kv = wuv_ref.shape[1] // 2
    parts = []
    for p in range(n_pairs):
        if head_major:
            parts.append(_dot(o_ref[0, 2 * p], wuv_ref[p, :kv, :])
                         + _dot(o_ref[0, 2 * p + 1], wuv_ref[p, kv:, :]))
        else:
            parts.append(_dot(o_ref[:, 2 * p * kv:(2 * p + 2) * kv], wuv_ref[p]))
    out_a = jnp.concatenate(parts, axis=-1).astype(MXU_DTYPE)
    merged = pbc_ref[...].astype(F32) + ga_ref[...].astype(F32) * _dot(out_a, wba_ref[...])
    x = x_ref[...] + _dot(merged.astype(MXU_DTYPE), wout_ref[...])
    out = _ffn_half_step(x, g_ref, wgu_ref, wd_ref, act_ref, chunk)
    if final:
        out = _rms(out) * fg_ref[...]
    out_ref[...] = out


def _mix_out_ffn(o_lat, x, pbc, ga, wuv_all, wb_all, w_out_all, g, wgu_all, wd_all, layer, final_g=None):
    n, d = x.shape
    ffn_dim = wd_all.shape[1]
    head_major = o_lat.ndim == 4
    final = final_g is not None
    row = lambda width: pl.BlockSpec((tm, width), lambda i: (i, 0))
    if head_major:
        _, heads, s, kv = o_lat.shape
        tm = min(OUT_ROW_TILE, s)
        assert s % tm == 0
        tps = s // tm
        o_spec = pl.BlockSpec((1, heads, tm, kv), lambda i: (i // tps, 0, i % tps, 0))
    else:
        tm = min(OUT_ROW_TILE, n)
        o_spec = row(o_lat.shape[1])
    assert n % tm == 0 and ffn_dim % FFN_COL_CHUNK == 0
    in_specs = [o_spec, row(d), row(d), row(d),
                _layer_spec(wuv_all, layer), _layer_spec(wb_all, layer, 0), _layer_spec(w_out_all, layer),
                _const_spec((1, d)), _layer_spec(wgu_all, layer), _layer_spec(wd_all, layer)]
    args = [o_lat, x, pbc, ga, wuv_all, wb_all, w_out_all, g.reshape(1, d), wgu_all, wd_all]
    if final:
        in_specs.append(_const_spec((1, d)))
        args.append(final_g.reshape(1, d))
    return pl.pallas_call(
        functools.partial(_mix_out_ffn_body, head_major=head_major, chunk=FFN_COL_CHUNK, final=final),
        grid=(n // tm,),
        in_specs=in_specs,
        out_specs=row(d),
        out_shape=jax.ShapeDtypeStruct((n, d), F32),
        scratch_shapes=[pltpu.VMEM((tm, ffn_dim), MXU_DTYPE)],
        compiler_params=_params(1),
        name="mix_out_ffn_final" if final else "mix_out_ffn",
    )(*args)


def _pair_block_diag(t):
    l, h, r, c = t.shape
    t = t.reshape(l, h // 2, 2, r, c)
    z = jnp.zeros((l, h // 2, r, c), t.dtype)
    top = jnp.concatenate([t[:, :, 0], z], axis=-1)
    bot = jnp.concatenate([z, t[:, :, 1]], axis=-1)
    return jnp.concatenate([top, bot], axis=-2)


def _rope_tables(pos, rope_dim):
    half = rope_dim // 2
    inv = ROPE_THETA ** (-jnp.arange(half, dtype=F32) / half)
    ang = pos.astype(F32)[:, None] * inv[None, :]
    cos, sin = jnp.cos(ang), jnp.sin(ang)
    reps = LANES // rope_dim
    return (jnp.tile(jnp.concatenate([cos, cos], axis=-1), (1, reps)),
            jnp.tile(jnp.concatenate([-sin, sin], axis=-1), (1, reps)))


def kernel(x_prompt, x_sample, cache_kv_latent, cache_k_rope, state_conv, page_table, ffn1_norm, ffn1_w_gate_up, ffn1_w_down, mix_norm, w_in, gate_bias, q_norm, w_uq, kv_norm, w_uk, w_uv, conv_w, sgu_ln_g, sgu_ln_b, sgu_w_spatial, sgu_b_spatial, w_branch, w_out, ffn2_norm, ffn2_w_gate_up, ffn2_w_down, final_norm):
    bp, s, d_model = x_prompt.shape
    bs, t_new, _ = x_sample.shape
    depth = w_in.shape[0]
    q_rank = q_norm.shape[-1]
    kv_rank = kv_norm.shape[-1]
    rope_dim = cache_k_rope.shape[-1]
    n_heads, nope_dim = w_uk.shape[2], w_uk.shape[3]
    v_dim = w_uv.shape[3]
    cw_dim = conv_w.shape[-1]
    conv_k = conv_w.shape[1]
    sgu_w = sgu_ln_g.shape[-1]
    n_groups = sgu_w_spatial.shape[1]
    group_dim = sgu_w // n_groups
    n_branch = w_branch.shape[1]
    past_len = page_table.shape[1] * cache_kv_latent.shape[2]
    assert conv_k == 3 and n_branch == 3 and s >= SGU_CHUNK and t_new < SGU_CHUNK
    assert w_uq.shape[3] == nope_dim + rope_dim and 2 * nope_dim == LANES and 2 * v_dim == LANES
    dims = (d_model, q_rank, kv_rank, rope_dim, cw_dim, sgu_w, n_heads, n_groups)
    cdt = MXU_DTYPE

    o_kr = q_rank + kv_rank
    w_in_p = jnp.concatenate(
        [w_in[..., :o_kr], w_in[..., o_kr + rope_dim:], w_in[..., o_kr:o_kr + rope_dim],
         jnp.zeros((depth, d_model, LANES - rope_dim), w_in.dtype)], axis=-1).astype(cdt)
    q_scale = (nope_dim + rope_dim) ** -0.5 * math.log2(math.e)
    wq_nope = w_uq[..., :nope_dim].reshape(depth, q_rank, n_heads * nope_dim).astype(cdt)
    wq_rope = w_uq[..., nope_dim:].reshape(depth, q_rank, n_heads * rope_dim).astype(cdt)
    wuk_pair = _pair_block_diag(w_uk.transpose(0, 2, 3, 1)).astype(cdt)
    wuv_pair = _pair_block_diag(w_uv.transpose(0, 2, 1, 3)).astype(cdt)
    wb = w_branch.astype(cdt)
    w_out_c = w_out.astype(cdt)
    f1gu, f1d = ffn1_w_gate_up.astype(cdt), ffn1_w_down.astype(cdt)
    f2gu, f2d = ffn2_w_gate_up.astype(cdt), ffn2_w_down.astype(cdt)

    bsp = jnp.repeat(sgu_b_spatial[:, :, :SGU_CHUNK].transpose(0, 2, 1), group_dim, axis=-1)
    w4 = jnp.tril(sgu_w_spatial[:, :, :t_new, :t_new])
    tabs = []
    for dlt in range(t_new):
        diag = jnp.diagonal(w4, offset=-dlt, axis1=2, axis2=3)
        diag = jnp.pad(diag, ((0, 0), (0, 0), (dlt, 0)))
        tabs.append(jnp.repeat(diag.transpose(0, 2, 1), group_dim, axis=-1))
    reps8 = SUBLANES // t_new
    tab_s = jnp.tile(jnp.stack(tabs, axis=1), (1, 1, reps8, 1))
    btab_s = jnp.tile(jnp.repeat(sgu_b_spatial[:, :, :t_new].transpose(0, 2, 1), group_dim, axis=-1),
                      (1, reps8, 1))

    cos_p, sin_p = _rope_tables(jnp.arange(s), rope_dim)
    cos_s, sin_s = _rope_tables(past_len + jnp.arange(t_new), rope_dim)
    cos_s, sin_s = jnp.tile(cos_s, (bs, 1)), jnp.tile(sin_s, (bs, 1))

    cache_rope_t = jnp.swapaxes(cache_k_rope, 2, 3)

    xp = x_prompt.reshape(bp * s, d_model)
    xs = x_sample.reshape(bs * t_new, d_model)
    qw = kv_rank + LANES
    lat_p, kr_p, conv_p, lat_s, kr_s, conv_s, v_s = [], [], [], [], [], [], []
    for l in range(depth):
        w = {"mix_norm": mix_norm[l].reshape(1, -1), "w_in": (w_in_p, (l,)), "q_norm": q_norm[l].reshape(1, -1),
             "wq_nope": (wq_nope, (l,)), "wq_rope": (wq_rope, (l,)), "wuk_pair": (wuk_pair, (l,)),
             "kv_norm": kv_norm[l].reshape(1, -1), "conv_w": conv_w[l],
             "ln_g": sgu_ln_g[l].reshape(1, -1), "ln_b": sgu_ln_b[l].reshape(1, -1),
             "wb_b": (wb, (l, 1)), "wb_c": (wb, (l, 2)), "gate_bias": gate_bias[l],
             "wsp": sgu_w_spatial[l][:, :SGU_CHUNK, :SGU_CHUNK], "bsp": bsp[l]}
        fin = final_norm if l == depth - 1 else None

        xp = _ffn(xp, ffn1_norm[l], f1gu, f1d, l)
        q, kk, ckv, kr, pbc, ga, nconv = _mix_in(xp, cos_p, sin_p, w, dims, s, q_scale)
        o = _attn_prompt(q, kk.reshape(bp, s, qw), kv_rank)
        xp = _mix_out_ffn(o, xp, pbc, ga, wuv_pair, wb, w_out_c, ffn2_norm[l], f2gu, f2d, l, fin)
        lat_p.append(ckv.reshape(bp, s, kv_rank))
        kr_p.append(kr.reshape(bp, s, rope_dim))
        conv_p.append(nconv)

        st = state_conv[l]
        s0 = jnp.repeat(st[:, 0], t_new, axis=0)
        s1 = jnp.repeat(st[:, 1], t_new, axis=0)
        xs = _ffn(xs, ffn1_norm[l], f1gu, f1d, l)
        q, kk, ckv, kr, pbc, ga, u, v = _mix_in(xs, cos_s, sin_s, w, dims, t_new, q_scale,
                                                sample_extra=(s0, s1, tab_s[l], btab_s[l]))
        o = _attn_sample(page_table, q.reshape(bs, t_new * n_heads, qw), kk.reshape(bs, t_new, qw),
                         cache_kv_latent, cache_rope_t, l, n_heads, kv_rank)
        xs = _mix_out_ffn(o.reshape(bs * t_new, n_heads * kv_rank), xs, pbc, ga, wuv_pair, wb, w_out_c,
                          ffn2_norm[l], f2gu, f2d, l, fin)
        lat_s.append(ckv.reshape(bs, t_new, kv_rank))
        kr_s.append(kr.reshape(bs, t_new, rope_dim))
        conv_s.append(u.reshape(bs, t_new, cw_dim)[:, t_new - (conv_k - 1):])
        v_s.append(v.reshape(bs, t_new, sgu_w))

    return (xp.reshape(bp, s, d_model), xs.reshape(bs, t_new, d_model),
            jnp.stack(lat_p), jnp.stack(kr_p), jnp.stack(conv_p),
            jnp.stack(lat_s), jnp.stack(kr_s), jnp.stack(conv_s), jnp.stack(v_s))
```
